```python
import jax
import jax.numpy as jnp
from jax import lax
import numpy as np

D_MODEL = 4096
BATCH = 2
SEQ = 8192
DEPTH = 2

D_FF = ((8 * D_MODEL // 3 + 255) // 256) * 256
SC_WIDTH = D_MODEL // 4
SC_KERNEL = 3
CF_WIDTH = D_MODEL // 4
CF_KERNEL = 31
RET_HEADS = 8
RET_WIDTH = D_MODEL // 2
RET_HEAD_DIM = RET_WIDTH // RET_HEADS
RET_CHUNK = 128
ROPE_BASE = 10000.0
EPS = 1e-6
IN_SIZES = (SC_WIDTH,) * 3 + (CF_WIDTH,) * 2 + (RET_WIDTH,) * 4 + (D_MODEL,) * 3
IN_WIDTH = sum(IN_SIZES)
SPLIT_POINTS = tuple(sum(IN_SIZES[:i + 1]) for i in range(len(IN_SIZES) - 1))

kernel_name = 'hybrid_gated_conv_retention_macaron'


def rms_norm(x, w):
    xf = x.astype(jnp.float32)
    y = xf * lax.rsqrt(jnp.mean(xf * xf, axis=-1, keepdims=True) + EPS)
    return (y * w.astype(jnp.float32)).astype(x.dtype)


def layer_norm(x, w, b):
    xf = x.astype(jnp.float32)
    mu = jnp.mean(xf, axis=-1, keepdims=True)
    var = jnp.mean(jnp.square(xf - mu), axis=-1, keepdims=True)
    y = (xf - mu) * lax.rsqrt(var + EPS)
    return (y * w.astype(jnp.float32) + b.astype(jnp.float32)).astype(x.dtype)


def swiglu(h, w_in, w_out):
    gate, up = jnp.split(h @ w_in, 2, axis=-1)
    return (jax.nn.silu(gate) * up) @ w_out


def causal_depthwise_conv(x, w):
    k = w.shape[0]
    return lax.conv_general_dilated(
        x, w[:, None, :].astype(x.dtype), window_strides=(1,), padding=[(k - 1, 0)],
        dimension_numbers=('NWC', 'WIO', 'NWC'), feature_group_count=x.shape[-1])


def rotary(t, positions):
    half = t.shape[-1] // 2
    inv_freq = ROPE_BASE ** (-jnp.arange(half, dtype=jnp.float32) / half)
    ang = positions.astype(jnp.float32)[..., None] * inv_freq
    cos = jnp.cos(ang)[:, :, None, :]
    sin = jnp.sin(ang)[:, :, None, :]
    t1, t2 = t[..., :half], t[..., half:]
    return jnp.concatenate([t1 * cos - t2 * sin, t1 * sin + t2 * cos], axis=-1)


def chunkwise_retention(q, k, v, log_gamma):
    b, s, h, dh = q.shape
    c = RET_CHUNK
    n = s // c
    idx = jnp.arange(c, dtype=jnp.float32)
    rel = idx[:, None] - idx[None, :]
    decay_mask = jnp.exp(jnp.where(rel[None] >= 0, rel[None] * log_gamma[:, None, None], -jnp.inf))
    query_decay = jnp.exp((idx[None, :] + 1.0) * log_gamma[:, None]).T[None, :, :, None]
    key_decay = jnp.exp((c - 1.0 - idx[None, :]) * log_gamma[:, None]).T[None, :, :, None]
    chunk_decay = jnp.exp(c * log_gamma)[None, :, None, None]

    def to_chunks(t):
        return t.reshape(b, n, c, h, dh).transpose(1, 0, 2, 3, 4)

    def step(state, inp):
        qc, kc, vc = inp
        scores = jnp.einsum('bihd,bjhd->bhij', qc, kc) * decay_mask[None]
        intra = jnp.einsum('bhij,bjhe->bihe', scores, vc)
        inter = jnp.einsum('bihd,bhde->bihe', qc, state) * query_decay
        new_state = state * chunk_decay + jnp.einsum('bjhd,bjhe->bhde', kc * key_decay, vc)
        return new_state, intra + inter

    state0 = jnp.zeros((b, h, dh, dh), jnp.float32)
    _, out = lax.scan(step, state0, (to_chunks(q), to_chunks(k), to_chunks(v)))
    return out.transpose(1, 0, 2, 3, 4).reshape(b, s, h, dh)


def hybrid_mixer(h, positions, w_in, sc_conv_w, cf_dw_w, cf_dw_b, cf_ln_w, cf_ln_b, ret_gn_w,
                 w_sc_out, w_cf_out, w_ret_out, w_mix_out):
    b, s, _ = h.shape
    (sc_b, sc_c, sc_h, cf_val, cf_gate, q, k, v, g,
     gate_sc, gate_cf, gate_ret) = jnp.split(h @ w_in, SPLIT_POINTS, axis=-1)

    y_sc = (sc_b * causal_depthwise_conv(sc_c * sc_h, sc_conv_w)) @ w_sc_out

    z = causal_depthwise_conv(cf_val * jax.nn.sigmoid(cf_gate), cf_dw_w) + cf_dw_b
    y_cf = jax.nn.silu(layer_norm(z, cf_ln_w, cf_ln_b)) @ w_cf_out

    log_gamma = jnp.log(1.0 - 2.0 ** (-5.0 - jnp.arange(RET_HEADS, dtype=jnp.float32)))
    qh = rotary(q.astype(jnp.float32).reshape(b, s, RET_HEADS, RET_HEAD_DIM), positions)
    kh = rotary(k.astype(jnp.float32).reshape(b, s, RET_HEADS, RET_HEAD_DIM), positions) * (RET_HEAD_DIM ** -0.5)
    vh = v.astype(jnp.float32).reshape(b, s, RET_HEADS, RET_HEAD_DIM)
    o = chunkwise_retention(qh, kh, vh, log_gamma)
    mu = jnp.mean(o, axis=-1, keepdims=True)
    var = jnp.mean(jnp.square(o - mu), axis=-1, keepdims=True)
    o = ((o - mu) * lax.rsqrt(var + EPS)).reshape(b, s, RET_WIDTH) * ret_gn_w.astype(jnp.float32)
    y_ret = (jax.nn.silu(g.astype(jnp.float32)) * o).astype(h.dtype) @ w_ret_out

    merged = (jax.nn.sigmoid(gate_sc) * y_sc + jax.nn.sigmoid(gate_cf) * y_cf
              + jax.nn.sigmoid(gate_ret) * y_ret)
    return merged @ w_mix_out


def setup_inputs(seed: int = 0) -> dict:
    key = jax.random.key(seed)
    ks = jax.random.split(key, 24)

    def dense(k, shape, fan_in):
        return jax.random.normal(k, shape, jnp.float32) * (fan_in ** -0.5)

    def gain(k, shape):
        return 1.0 + 0.02 * jax.random.normal(k, shape, jnp.float32)

    def bias(k, shape):
        return 0.02 * jax.random.normal(k, shape, jnp.float32)

    x = jax.random.normal(ks[0], (BATCH, SEQ, D_MODEL), jnp.float32)
    offsets = jax.random.randint(ks[1], (BATCH, 1), 0, 1024, dtype=jnp.int32)
    positions = offsets + jnp.arange(SEQ, dtype=jnp.int32)[None, :]
    return {
        'x': x,
        'positions': positions,
        'norm_ffn1': gain(ks[2], (DEPTH, D_MODEL)),
        'ffn1_in': dense(ks[3], (DEPTH, D_MODEL, 2 * D_FF), D_MODEL),
        'ffn1_out': dense(ks[4], (DEPTH, D_FF, D_MODEL), D_FF),
        'norm_mix': gain(ks[5], (DEPTH, D_MODEL)),
        'w_in': dense(ks[6], (DEPTH, D_MODEL, IN_WIDTH), D_MODEL),
        'sc_conv_w': dense(ks[7], (DEPTH, SC_KERNEL, SC_WIDTH), SC_KERNEL),
        'cf_dw_w': dense(ks[8], (DEPTH, CF_KERNEL, CF_WIDTH), CF_KERNEL),
        'cf_dw_b': bias(ks[9], (DEPTH, CF_WIDTH)),
        'cf_ln_w': gain(ks[10], (DEPTH, CF_WIDTH)),
        'cf_ln_b': bias(ks[11], (DEPTH, CF_WIDTH)),
        'ret_gn_w': gain(ks[12], (DEPTH, RET_WIDTH)),
        'w_sc_out': dense(ks[13], (DEPTH, SC_WIDTH, D_MODEL), SC_WIDTH),
        'w_cf_out': dense(ks[14], (DEPTH, CF_WIDTH, D_MODEL), CF_WIDTH),
        'w_ret_out': dense(ks[15], (DEPTH, RET_WIDTH, D_MODEL), RET_WIDTH),
        'w_mix_out': dense(ks[16], (DEPTH, D_MODEL, D_MODEL), D_MODEL),
        'norm_ffn2': gain(ks[17], (DEPTH, D_MODEL)),
        'ffn2_in': dense(ks[18], (DEPTH, D_MODEL, 2 * D_FF), D_MODEL),
        'ffn2_out': dense(ks[19], (DEPTH, D_FF, D_MODEL), D_FF),
        'norm_final': gain(ks[20], (D_MODEL,)),
    }


def reference(x, positions, norm_ffn1, ffn1_in, ffn1_out, norm_mix, w_in, sc_conv_w, cf_dw_w,
              cf_dw_b, cf_ln_w, cf_ln_b, ret_gn_w, w_sc_out, w_cf_out, w_ret_out, w_mix_out,
              norm_ffn2, ffn2_in, ffn2_out, norm_final):
    for l in range(DEPTH):
        x = x + 0.5 * swiglu(rms_norm(x, norm_ffn1[l]), ffn1_in[l], ffn1_out[l])
        x = x + hybrid_mixer(rms_norm(x, norm_mix[l]), positions, w_in[l], sc_conv_w[l], cf_dw_w[l],
                             cf_dw_b[l], cf_ln_w[l], cf_ln_b[l], ret_gn_w[l], w_sc_out[l],
                             w_cf_out[l], w_ret_out[l], w_mix_out[l])
        x = x + 0.5 * swiglu(rms_norm(x, norm_ffn2[l]), ffn2_in[l], ffn2_out[l])
    return rms_norm(x, norm_final)
```

```python
import functools
import math

import jax
import jax.numpy as jnp
from jax import lax
from jax.experimental import pallas as pl
from jax.experimental.pallas import tpu as pltpu

EPS = 1e-6
ROPE_BASE = 10000.0
RET_HEADS = 8
RET_CHUNK = 128
SC_KERNEL = 3
CF_KERNEL = 31
SC_HALO = 8
CF_HALO = 32
LANES = 128
NORM_ROWS = 64
VMEM_LIMIT = 58 * 1024 * 1024

F32 = jnp.float32
BF16 = jnp.bfloat16


def _sigmoid(x):
    return 1.0 / (1.0 + jnp.exp(-x))


def _silu(x):
    return x * _sigmoid(x)


def _rms_rows(x, gain):
    return (x * lax.rsqrt(jnp.mean(x * x, axis=-1, keepdims=True) + EPS)) * gain


def _norm_prologue(x_ref, gain_ref, xn_ref, copy_ref=None):
    rows = x_ref.shape[0]

    def body(i, carry):
        r = pl.ds(pl.multiple_of(i * NORM_ROWS, NORM_ROWS), NORM_ROWS)
        x = x_ref[r, :]
        xn_ref[r, :] = _rms_rows(x, gain_ref[...]).astype(BF16)
        if copy_ref is not None:
            copy_ref[r, :] = x
        return carry

    lax.fori_loop(0, rows // NORM_ROWS, body, 0)


def _ffn_kernel(x_ref, gain_ref, wg_ref, wu_ref, wo_ref, fgain_ref, o_ref, xn_ref, *, final_norm):
    j = pl.program_id(1)

    @pl.when(j == 0)
    def _():
        _norm_prologue(x_ref, gain_ref, xn_ref, copy_ref=o_ref)

    xn = xn_ref[...]
    g = jnp.dot(xn, wg_ref[...], preferred_element_type=F32)
    u = jnp.dot(xn, wu_ref[...], preferred_element_type=F32)
    a = (0.5 * _silu(g) * u).astype(BF16)
    o_ref[...] += jnp.dot(a, wo_ref[...], preferred_element_type=F32)

    if final_norm:
        @pl.when(j == pl.num_programs(1) - 1)
        def _():
            def body(i, carry):
                r = pl.ds(pl.multiple_of(i * NORM_ROWS, NORM_ROWS), NORM_ROWS)
                o_ref[r, :] = _rms_rows(o_ref[r, :], fgain_ref[...])
                return carry
            lax.fori_loop(0, o_ref.shape[0] // NORM_ROWS, body, 0)


def _ffn(x, gain, w_in, w_out, final_gain, *, final_norm, bm, bf):
    m, d = x.shape
    f = w_out.shape[0]
    nf = f // bf
    kern = functools.partial(_ffn_kernel, final_norm=final_norm)
    return pl.pallas_call(
        kern,
        grid=(m // bm, nf),
        in_specs=[
            pl.BlockSpec((bm, d), lambda i, j: (i, 0)),
            pl.BlockSpec((1, d), lambda i, j: (0, 0)),
            pl.BlockSpec((d, bf), lambda i, j: (0, j)),
            pl.BlockSpec((d, bf), lambda i, j: (0, j + nf)),
            pl.BlockSpec((bf, d), lambda i, j: (j, 0)),
            pl.BlockSpec((1, d), lambda i, j: (0, 0)),
        ],
        out_specs=pl.BlockSpec((bm, d), lambda i, j: (i, 0)),
        out_shape=jax.ShapeDtypeStruct((m, d), F32),
        scratch_shapes=[pltpu.VMEM((bm, d), BF16)],
        compiler_params=pltpu.CompilerParams(
            dimension_semantics=("arbitrary", "arbitrary"), vmem_limit_bytes=VMEM_LIMIT),
        name="ffn_final" if final_norm else "ffn",
    )(x, gain.reshape(1, d), w_in, w_in, w_out, final_gain.reshape(1, d))


def _inproj_kernel(x_ref, gain_ref, w_ref, o_ref, xn_ref):
    @pl.when(pl.program_id(1) == 0)
    def _():
        _norm_prologue(x_ref, gain_ref, xn_ref)

    o_ref[...] = jnp.dot(xn_ref[...], w_ref[...], preferred_element_type=F32).astype(o_ref.dtype)


def _inproj(x, gain, w, *, bm, bn):
    m, d = x.shape
    n = w.shape[1]
    return pl.pallas_call(
        _inproj_kernel,
        grid=(m // bm, n // bn),
        in_specs=[
            pl.BlockSpec((bm, d), lambda i, j: (i, 0)),
            pl.BlockSpec((1, d), lambda i, j: (0, 0)),
            pl.BlockSpec((d, bn), lambda i, j: (0, j)),
        ],
        out_specs=pl.BlockSpec((bm, bn), lambda i, j: (i, j)),
        out_shape=jax.ShapeDtypeStruct((m, n), BF16),
        scratch_shapes=[pltpu.VMEM((bm, d), BF16)],
        compiler_params=pltpu.CompilerParams(
            dimension_semantics=("arbitrary", "arbitrary"), vmem_limit_bytes=VMEM_LIMIT),
        name="inproj",
    )(x, gain.reshape(1, d), w)


def _mixer_kernel(*refs, ts, cw, hd, nq):
    (scb_ref, scc_ref, sch_ref, cfv_ref, cfg_ref) = refs[:5]
    q_refs = refs[5:5 + nq]
    k_refs = refs[5 + nq:5 + 2 * nq]
    v_refs = refs[5 + 2 * nq:5 + 3 * nq]
    g_refs = refs[5 + 3 * nq:5 + 4 * nq]
    (pos_ref, invf_ref, scw_ref, cfw_ref, cfb_ref, lnw_ref, lnb_ref, gnw_ref,
     o_ref, sbuf, cbuf, state) = refs[5 + 4 * nq:]
    half = hd // 2
    heads_per_ref = cw // hd

    @pl.when(pl.program_id(1) == 0)
    def _():
        sbuf[0:SC_HALO, :] = jnp.zeros((SC_HALO, cw), F32)
        cbuf[0:CF_HALO, :] = jnp.zeros((CF_HALO, cw), F32)
        state[...] = jnp.zeros(state.shape, F32)

    rb = 64

    sbuf[SC_HALO:SC_HALO + ts, :] = scc_ref[...].astype(F32) * sch_ref[...].astype(F32)
    for r0 in range(0, ts, rb):
        for c0 in range(0, cw, LANES):
            cs = slice(c0, c0 + LANES)
            acc = None
            for j in range(SC_KERNEL):
                off = SC_HALO - (SC_KERNEL - 1) + j + r0
                term = scw_ref[j:j + 1, cs] * sbuf[off:off + rb, cs]
                acc = term if acc is None else acc + term
            o_ref[r0:r0 + rb, cs] = (scb_ref[r0:r0 + rb, cs].astype(F32) * acc).astype(o_ref.dtype)
    sbuf[0:SC_HALO, :] = sbuf[ts:ts + SC_HALO, :]

    cbuf[CF_HALO:CF_HALO + ts, :] = cfv_ref[...].astype(F32) * _sigmoid(cfg_ref[...].astype(F32))
    for r0 in range(0, ts, rb):
        zs = []
        for c0 in range(0, cw, LANES):
            cs = slice(c0, c0 + LANES)
            acc = None
            for j in range(CF_KERNEL):
                off = CF_HALO - (CF_KERNEL - 1) + j + r0
                term = cfw_ref[j:j + 1, cs] * cbuf[off:off + rb, cs]
                acc = term if acc is None else acc + term
            zs.append(acc + cfb_ref[:, cs])
        z = jnp.concatenate(zs, axis=-1)
        mu = jnp.mean(z, axis=-1, keepdims=True)
        zc = z - mu
        var = jnp.mean(zc * zc, axis=-1, keepdims=True)
        y = zc * lax.rsqrt(var + EPS) * lnw_ref[...] + lnb_ref[...]
        o_ref[r0:r0 + rb, cw:2 * cw] = _silu(y).astype(o_ref.dtype)
    cbuf[0:CF_HALO, :] = cbuf[ts:ts + CF_HALO, :]

    ang = pos_ref[...] * invf_ref[...]
    cos = jnp.cos(ang)
    sin = jnp.sin(ang)
    c = RET_CHUNK
    row = lax.broadcasted_iota(jnp.int32, (c, c), 0).astype(F32)
    col = lax.broadcasted_iota(jnp.int32, (c, c), 1).astype(F32)
    rel = row - col
    rowd = lax.broadcasted_iota(jnp.int32, (c, hd), 0).astype(F32)
    k_scale = float(hd) ** -0.5

    def rot(t, cs_, sn_):
        t1, t2 = t[:, :half], t[:, half:]
        return jnp.concatenate([t1 * cs_ - t2 * sn_, t1 * sn_ + t2 * cs_], axis=-1)

    for h in range(RET_HEADS):
        lg = math.log(1.0 - 2.0 ** (-5.0 - h))
        mask = jnp.exp(jnp.where(rel >= 0, rel * lg, -jnp.inf))
        qd = jnp.exp((rowd + 1.0) * lg)
        kd = jnp.exp((c - 1.0 - rowd) * lg)
        cd = math.exp(c * lg)
        ri, hl = divmod(h, heads_per_ref)
        hs = slice(hl * hd, (hl + 1) * hd)
        for ci in range(ts // c):
            rs = slice(ci * c, (ci + 1) * c)
            cs_, sn_ = cos[rs], sin[rs]
            q = rot(q_refs[ri][rs, hs].astype(F32), cs_, sn_)
            k = rot(k_refs[ri][rs, hs].astype(F32), cs_, sn_) * k_scale
            v = v_refs[ri][rs, hs]
            qb = q.astype(BF16)
            st = state[h]
            scores = lax.dot_general(qb, k.astype(BF16), (((1,), (1,)), ((), ())),
                                     preferred_element_type=F32) * mask
            intra = jnp.dot(scores.astype(BF16), v, preferred_element_type=F32)
            inter = jnp.dot(qb, st.astype(BF16), preferred_element_type=F32) * qd
            state[h] = st * cd + lax.dot_general((k * kd).astype(BF16), v, (((0,), (0,)), ((), ())),
                                                 preferred_element_type=F32)
            o = intra + inter
            mu = jnp.mean(o, axis=-1, keepdims=True)
            oc = o - mu
            var = jnp.mean(oc * oc, axis=-1, keepdims=True)
            on = oc * lax.rsqrt(var + EPS) * gnw_ref[:, h * hd:(h + 1) * hd]
            gate = g_refs[ri][rs, hs].astype(F32)
            o_ref[rs, 2 * cw + h * hd:2 * cw + (h + 1) * hd] = (_silu(gate) * on).astype(o_ref.dtype)


def _mixer(h3, posb, invf, sc_w, cf_w, cf_b, ln_w, ln_b, gn_w, *, ts, cw, rw):
    b, s, n = h3.shape
    hd = rw // RET_HEADS
    nq = rw // cw
    ncol = 5 + 4 * nq

    def hspec(col):
        return pl.BlockSpec((None, ts, cw), lambda bi, si: (bi, si, col))

    def full(a):
        return pl.BlockSpec(a.shape, lambda bi, si: (0,) * a.ndim)

    small = [invf, sc_w, cf_w, cf_b, ln_w, ln_b, gn_w]
    kern = functools.partial(_mixer_kernel, ts=ts, cw=cw, hd=hd, nq=nq)
    return pl.pallas_call(
        kern,
        grid=(b, s // ts),
        in_specs=[hspec(cidx) for cidx in range(ncol)]
        + [pl.BlockSpec((None, ts, LANES), lambda bi, si: (bi, si, 0))]
        + [full(a) for a in small],
        out_specs=pl.BlockSpec((None, ts, 2 * cw + rw), lambda bi, si: (bi, si, 0)),
        out_shape=jax.ShapeDtypeStruct((b, s, 2 * cw + rw), BF16),
        scratch_shapes=[
            pltpu.VMEM((SC_HALO + ts, cw), F32),
            pltpu.VMEM((CF_HALO + ts, cw), F32),
            pltpu.VMEM((RET_HEADS, hd, hd), F32),
        ],
        compiler_params=pltpu.CompilerParams(
            dimension_semantics=("arbitrary", "arbitrary"), vmem_limit_bytes=VMEM_LIMIT),
        name="mixer",
    )(*([h3] * ncol), posb, *small)


def _merge_kernel(x_ref, a_ref, gsc_ref, gcf_ref, gret_ref, wsc_ref, wcf_ref, wret_ref, wmix_ref,
                  o_ref, *, cw):
    @pl.when(pl.program_id(1) == 0)
    def _():
        o_ref[...] = x_ref[...]

    y_sc = jnp.dot(a_ref[:, 0:cw], wsc_ref[...], preferred_element_type=F32)
    y_cf = jnp.dot(a_ref[:, cw:2 * cw], wcf_ref[...], preferred_element_type=F32)
    y_ret = jnp.dot(a_ref[:, 2 * cw:], wret_ref[...], preferred_element_type=F32)
    merged = (_sigmoid(gsc_ref[...].astype(F32)) * y_sc + _sigmoid(gcf_ref[...].astype(F32)) * y_cf
              + _sigmoid(gret_ref[...].astype(F32)) * y_ret)
    o_ref[...] += jnp.dot(merged.astype(BF16), wmix_ref[...], preferred_element_type=F32)


def _merge(x, a, h, w_sc, w_cf, w_ret, w_mix, *, gate_col0, bm, bn):
    m, d = x.shape
    cw = w_sc.shape[0]
    rw = w_ret.shape[0]
    g0 = gate_col0 // bn
    gstep = d // bn

    def gspec(kidx):
        return pl.BlockSpec((bm, bn), lambda i, j: (i, g0 + kidx * gstep + j))

    kern = functools.partial(_merge_kernel, cw=cw)
    return pl.pallas_call(
        kern,
        grid=(m // bm, d // bn),
        in_specs=[
            pl.BlockSpec((bm, d), lambda i, j: (i, 0)),
            pl.BlockSpec((bm, 2 * cw + rw), lambda i, j: (i, 0)),
            gspec(0), gspec(1), gspec(2),
            pl.BlockSpec((cw, bn), lambda i, j: (0, j)),
            pl.BlockSpec((cw, bn), lambda i, j: (0, j)),
            pl.BlockSpec((rw, bn), lambda i, j: (0, j)),
            pl.BlockSpec((bn, d), lambda i, j: (j, 0)),
        ],
        out_specs=pl.BlockSpec((bm, d), lambda i, j: (i, 0)),
        out_shape=jax.ShapeDtypeStruct((m, d), F32),
        compiler_params=pltpu.CompilerParams(
            dimension_semantics=("arbitrary", "arbitrary"), vmem_limit_bytes=VMEM_LIMIT),
        name="merge",
    )(x, a, h, h, h, w_sc, w_cf, w_ret, w_mix)


def _tile(n, pref):
    return pref if n % pref == 0 else n


def kernel(x, positions, norm_ffn1, ffn1_in, ffn1_out, norm_mix, w_in, sc_conv_w, cf_dw_w, cf_dw_b, cf_ln_w, cf_ln_b, ret_gn_w, w_sc_out, w_cf_out, w_ret_out, w_mix_out, norm_ffn2, ffn2_in, ffn2_out, norm_final):
    b, s, d = x.shape
    m = b * s
    depth = w_in.shape[0]
    cw = w_sc_out.shape[1]
    rw = w_ret_out.shape[1]
    hd = rw // RET_HEADS
    half = hd // 2
    assert w_cf_out.shape[1] == cw and rw % cw == 0 and half == LANES
    gate_col0 = 5 * cw + 4 * rw
    assert w_in.shape[2] == gate_col0 + 3 * d

    bm = _tile(m, 512)
    bf = _tile(ffn1_out.shape[1], 256)
    bm_in = _tile(m, 1024)
    bn_in = _tile(w_in.shape[2], 512)
    bn_mg = _tile(d, 256)
    ts = _tile(s, 256)

    inv_freq = (ROPE_BASE ** (-jnp.arange(half, dtype=F32) / half)).reshape(1, half)
    posb = jnp.broadcast_to(positions.astype(F32)[..., None], (b, s, LANES))

    xf = x.reshape(m, d)
    for l in range(depth):
        xf = _ffn(xf, norm_ffn1[l], ffn1_in[l].astype(BF16), ffn1_out[l].astype(BF16), norm_final,
                  final_norm=False, bm=bm, bf=bf)
        h = _inproj(xf, norm_mix[l], w_in[l].astype(BF16), bm=bm_in, bn=bn_in)
        a = _mixer(h.reshape(b, s, -1), posb, inv_freq, sc_conv_w[l], cf_dw_w[l],
                   cf_dw_b[l].reshape(1, cw), cf_ln_w[l].reshape(1, cw), cf_ln_b[l].reshape(1, cw),
                   ret_gn_w[l].reshape(1, rw), ts=ts, cw=cw, rw=rw)
        xf = _merge(xf, a.reshape(m, -1), h, w_sc_out[l].astype(BF16), w_cf_out[l].astype(BF16),
                    w_ret_out[l].astype(BF16), w_mix_out[l].astype(BF16),
                    gate_col0=gate_col0, bm=bm, bn=bn_mg)
        xf = _ffn(xf, norm_ffn2[l], ffn2_in[l].astype(BF16), ffn2_out[l].astype(BF16), norm_final,
                  final_norm=(l == depth - 1), bm=bm, bf=bf)
    return xf.reshape(b, s, d)
```

```python
import functools
import math

import jax
import jax.numpy as jnp
from jax import lax
from jax.experimental import pallas as pl
from jax.experimental.pallas import tpu as pltpu

EPS = 1e-6
ROPE_BASE = 10000.0
RET_HEADS = 8
RET_CHUNK = 128
SC_KERNEL = 3
CF_KERNEL = 31
SUBLANES = 8
LANES = 128
SC_HALO = 8
CF_HALO = 32
CONV_ROWS = 64
NORM_ROWS = 64
FETCH_ROWS = 128
VMEM_LIMIT = 58 * 1024 * 1024

F32 = jnp.float32
BF16 = jnp.bfloat16


def _sigmoid(x):
    return 1.0 / (1.0 + jnp.exp(-x))


def _silu(x):
    return x * _sigmoid(x)


def _rms_rows(x, gain):
    return (x * lax.rsqrt(jnp.mean(x * x, axis=-1, keepdims=True) + EPS)) * gain


def _row_copy(x_hbm, dst_ref, sems, row0, c):
    return pltpu.make_async_copy(
        x_hbm.at[pl.ds(row0 + c * FETCH_ROWS, FETCH_ROWS), :],
        dst_ref.at[pl.ds(c * FETCH_ROWS, FETCH_ROWS), :],
        sems.at[c])


def _fetch_rows_start(x_hbm, dst_ref, sems, row0):
    for c in range(dst_ref.shape[0] // FETCH_ROWS):
        _row_copy(x_hbm, dst_ref, sems, row0, c).start()


def _fetch_rows_wait(x_hbm, dst_ref, sems, row0):
    for c in range(dst_ref.shape[0] // FETCH_ROWS):
        _row_copy(x_hbm, dst_ref, sems, row0, c).wait()


def _fetch_and_norm(x_hbm, dst_ref, sems, row0, gain_ref, xn_ref):
    _fetch_rows_start(x_hbm, dst_ref, sems, row0)
    for c in range(dst_ref.shape[0] // FETCH_ROWS):
        _row_copy(x_hbm, dst_ref, sems, row0, c).wait()
        for r0 in range(c * FETCH_ROWS, (c + 1) * FETCH_ROWS, NORM_ROWS):
            rows = slice(r0, r0 + NORM_ROWS)
            xn_ref[rows, :] = _rms_rows(dst_ref[rows, :], gain_ref[...]).astype(BF16)


def _params():
    return pltpu.CompilerParams(
        dimension_semantics=("arbitrary", "arbitrary"), vmem_limit_bytes=VMEM_LIMIT)


def _ffn_kernel(x_hbm, gain_ref, wg_ref, wu_ref, wo_ref, fgain_ref, o_ref, xn_ref, sems, *, final_norm):
    j = pl.program_id(1)
    row0 = pl.program_id(0) * o_ref.shape[0]

    @pl.when(j == 0)
    def _():
        _fetch_and_norm(x_hbm, o_ref, sems, row0, gain_ref, xn_ref)

    xn = xn_ref[...]
    g = jnp.dot(xn, wg_ref[...], preferred_element_type=F32)
    u = jnp.dot(xn, wu_ref[...], preferred_element_type=F32)
    a = (0.5 * _silu(g) * u).astype(BF16)
    o_ref[...] += jnp.dot(a, wo_ref[...], preferred_element_type=F32)

    if final_norm:
        @pl.when(j == pl.num_programs(1) - 1)
        def _():
            def body(i, carry):
                r = pl.ds(pl.multiple_of(i * NORM_ROWS, NORM_ROWS), NORM_ROWS)
                o_ref[r, :] = _rms_rows(o_ref[r, :], fgain_ref[...])
                return carry
            lax.fori_loop(0, o_ref.shape[0] // NORM_ROWS, body, 0)


def _ffn(x, gain, w_in, w_out, layer, final_gain, *, final_norm, bm, bf):
    m, d = x.shape
    f = w_out.shape[1]
    nf = f // bf
    kern = functools.partial(_ffn_kernel, final_norm=final_norm)
    return pl.pallas_call(
        kern,
        grid=(m // bm, nf),
        in_specs=[
            pl.BlockSpec(memory_space=pl.ANY),
            pl.BlockSpec((1, d), lambda i, j: (0, 0)),
            pl.BlockSpec((None, d, bf), lambda i, j: (layer, 0, j)),
            pl.BlockSpec((None, d, bf), lambda i, j: (layer, 0, j + nf)),
            pl.BlockSpec((None, bf, d), lambda i, j: (layer, j, 0)),
            pl.BlockSpec((1, d), lambda i, j: (0, 0)),
        ],
        out_specs=pl.BlockSpec((bm, d), lambda i, j: (i, 0)),
        out_shape=jax.ShapeDtypeStruct((m, d), F32),
        scratch_shapes=[pltpu.VMEM((bm, d), BF16), pltpu.SemaphoreType.DMA((bm // FETCH_ROWS,))],
        compiler_params=_params(),
        name="ffn_final" if final_norm else "ffn",
    )(x, gain.reshape(1, d), w_in, w_in, w_out, final_gain.reshape(1, d))


def _inproj_kernel(x_hbm, gain_ref, w_ref, o_ref, x_ref, xn_ref, sems):
    row0 = pl.program_id(0) * x_ref.shape[0]

    @pl.when(pl.program_id(1) == 0)
    def _():
        _fetch_and_norm(x_hbm, x_ref, sems, row0, gain_ref, xn_ref)

    o_ref[...] = jnp.dot(xn_ref[...], w_ref[...], preferred_element_type=F32).astype(o_ref.dtype)


def _inproj(x, gain, w, layer, *, bm, bn):
    m, d = x.shape
    n = w.shape[2]
    return pl.pallas_call(
        _inproj_kernel,
        grid=(m // bm, n // bn),
        in_specs=[
            pl.BlockSpec(memory_space=pl.ANY),
            pl.BlockSpec((1, d), lambda i, j: (0, 0)),
            pl.BlockSpec((None, d, bn), lambda i, j: (layer, 0, j)),
        ],
        out_specs=pl.BlockSpec((bm, bn), lambda i, j: (i, j)),
        out_shape=jax.ShapeDtypeStruct((m, n), BF16),
        scratch_shapes=[pltpu.VMEM((bm, d), F32), pltpu.VMEM((bm, d), BF16),
                        pltpu.SemaphoreType.DMA((bm // FETCH_ROWS,))],
        compiler_params=_params(),
        name="inproj",
    )(x, gain.reshape(1, d), w)


def _causal_conv_block(buf, w_ref, r0, cs, halo, ktaps):
    span = -(-(ktaps - 1) // SUBLANES) * SUBLANES
    base = halo + r0 - span
    win = buf[base:base + span + CONV_ROWS, cs]
    lead = span - (ktaps - 1)
    acc = None
    for phase in range(SUBLANES):
        taps = [j for j in range(ktaps) if (lead + j) % SUBLANES == phase]
        if not taps:
            continue
        shifted = win[phase:phase + span + CONV_ROWS - SUBLANES] if phase else win
        for j in taps:
            a0 = lead + j - phase
            term = w_ref[j:j + 1, cs] * shifted[a0:a0 + CONV_ROWS]
            acc = term if acc is None else acc + term
    return acc


def _mixer_kernel(*refs, ts, cw, hd, nq):
    (scb_ref, scc_ref, sch_ref, cfv_ref, cfg_ref) = refs[:5]
    q_refs = refs[5:5 + nq]
    k_refs = refs[5 + nq:5 + 2 * nq]
    v_refs = refs[5 + 2 * nq:5 + 3 * nq]
    g_refs = refs[5 + 3 * nq:5 + 4 * nq]
    (pos_ref, invf_ref, scw_ref, cfw_ref, cfb_ref, lnw_ref, lnb_ref, gnw_ref,
     o_ref, sbuf, cbuf, state) = refs[5 + 4 * nq:]
    half = hd // 2
    heads_per_ref = cw // hd

    @pl.when(pl.program_id(1) == 0)
    def _():
        sbuf[0:SC_HALO, :] = jnp.zeros((SC_HALO, cw), F32)
        cbuf[0:CF_HALO, :] = jnp.zeros((CF_HALO, cw), F32)
        state[...] = jnp.zeros(state.shape, F32)

    sbuf[SC_HALO:SC_HALO + ts, :] = scc_ref[...].astype(F32) * sch_ref[...].astype(F32)
    for r0 in range(0, ts, CONV_ROWS):
        rows = slice(r0, r0 + CONV_ROWS)
        for c0 in range(0, cw, LANES):
            cs = slice(c0, c0 + LANES)
            y = _causal_conv_block(sbuf, scw_ref, r0, cs, SC_HALO, SC_KERNEL)
            o_ref[rows, cs] = (scb_ref[rows, cs].astype(F32) * y).astype(o_ref.dtype)
    sbuf[0:SC_HALO, :] = sbuf[ts:ts + SC_HALO, :]

    cbuf[CF_HALO:CF_HALO + ts, :] = cfv_ref[...].astype(F32) * _sigmoid(cfg_ref[...].astype(F32))
    for r0 in range(0, ts, CONV_ROWS):
        zs = []
        for c0 in range(0, cw, LANES):
            cs = slice(c0, c0 + LANES)
            zs.append(_causal_conv_block(cbuf, cfw_ref, r0, cs, CF_HALO, CF_KERNEL) + cfb_ref[:, cs])
        z = jnp.concatenate(zs, axis=-1)
        mu = jnp.mean(z, axis=-1, keepdims=True)
        zc = z - mu
        var = jnp.mean(zc * zc, axis=-1, keepdims=True)
        y = zc * lax.rsqrt(var + EPS) * lnw_ref[...] + lnb_ref[...]
        o_ref[r0:r0 + CONV_ROWS, cw:2 * cw] = _silu(y).astype(o_ref.dtype)
    cbuf[0:CF_HALO, :] = cbuf[ts:ts + CF_HALO, :]

    ang = pos_ref[...] * invf_ref[...]
    cos = jnp.cos(ang)
    sin = jnp.sin(ang)
    c = RET_CHUNK
    row = lax.broadcasted_iota(jnp.int32, (c, c), 0).astype(F32)
    col = lax.broadcasted_iota(jnp.int32, (c, c), 1).astype(F32)
    rel = row - col
    rowd = lax.broadcasted_iota(jnp.int32, (c, hd), 0).astype(F32)
    k_scale = float(hd) ** -0.5

    def rot(t, cs_, sn_):
        t1, t2 = t[:, :half], t[:, half:]
        return jnp.concatenate([t1 * cs_ - t2 * sn_, t1 * sn_ + t2 * cs_], axis=-1)

    for h in range(RET_HEADS):
        lg = math.log(1.0 - 2.0 ** (-5.0 - h))
        mask = jnp.exp(jnp.where(rel >= 0, rel * lg, -jnp.inf))
        qd = jnp.exp((rowd + 1.0) * lg)
        kd = jnp.exp((c - 1.0 - rowd) * lg)
        cd = math.exp(c * lg)
        ri, hl = divmod(h, heads_per_ref)
        hs = slice(hl * hd, (hl + 1) * hd)
        for ci in range(ts // c):
            rs = slice(ci * c, (ci + 1) * c)
            cs_, sn_ = cos[rs], sin[rs]
            q = rot(q_refs[ri][rs, hs].astype(F32), cs_, sn_)
            k = rot(k_refs[ri][rs, hs].astype(F32), cs_, sn_) * k_scale
            v = v_refs[ri][rs, hs]
            qb = q.astype(BF16)
            st = state[h]
            scores = lax.dot_general(qb, k.astype(BF16), (((1,), (1,)), ((), ())),
                                     preferred_element_type=F32) * mask
            intra = jnp.dot(scores.astype(BF16), v, preferred_element_type=F32)
            inter = jnp.dot(qb, st.astype(BF16), preferred_element_type=F32) * qd
            state[h] = st * cd + lax.dot_general((k * kd).astype(BF16), v, (((0,), (0,)), ((), ())),
                                                 preferred_element_type=F32)
            o = intra + inter
            mu = jnp.mean(o, axis=-1, keepdims=True)
            oc = o - mu
            var = jnp.mean(oc * oc, axis=-1, keepdims=True)
            on = oc * lax.rsqrt(var + EPS) * gnw_ref[:, h * hd:(h + 1) * hd]
            gate = g_refs[ri][rs, hs].astype(F32)
            o_ref[rs, 2 * cw + h * hd:2 * cw + (h + 1) * hd] = (_silu(gate) * on).astype(o_ref.dtype)


def _mixer(h3, posb, invf, sc_w, cf_w, cf_b, ln_w, ln_b, gn_w, *, ts, cw, rw):
    b, s, n = h3.shape
    hd = rw // RET_HEADS
    nq = rw // cw
    ncol = 5 + 4 * nq

    def hspec(col):
        return pl.BlockSpec((None, ts, cw), lambda bi, si: (bi, si, col))

    def full(a):
        return pl.BlockSpec(a.shape, lambda bi, si: (0,) * a.ndim)

    small = [invf, sc_w, cf_w, cf_b, ln_w, ln_b, gn_w]
    kern = functools.partial(_mixer_kernel, ts=ts, cw=cw, hd=hd, nq=nq)
    return pl.pallas_call(
        kern,
        grid=(b, s // ts),
        in_specs=[hspec(cidx) for cidx in range(ncol)]
        + [pl.BlockSpec((None, ts, LANES), lambda bi, si: (bi, si, 0))]
        + [full(a) for a in small],
        out_specs=pl.BlockSpec((None, ts, 2 * cw + rw), lambda bi, si: (bi, si, 0)),
        out_shape=jax.ShapeDtypeStruct((b, s, 2 * cw + rw), BF16),
        scratch_shapes=[
            pltpu.VMEM((SC_HALO + ts, cw), F32),
            pltpu.VMEM((CF_HALO + ts, cw), F32),
            pltpu.VMEM((RET_HEADS, hd, hd), F32),
        ],
        compiler_params=_params(),
        name="mixer",
    )(*([h3] * ncol), posb, *small)


def _merge_kernel(x_hbm, a_ref, gsc_ref, gcf_ref, gret_ref, wsc_ref, wcf_ref, wret_ref, wmix_ref,
                  o_ref, sems, *, cw):
    first = pl.program_id(1) == 0
    row0 = pl.program_id(0) * o_ref.shape[0]

    @pl.when(first)
    def _():
        _fetch_rows_start(x_hbm, o_ref, sems, row0)

    y_sc = jnp.dot(a_ref[:, 0:cw], wsc_ref[...], preferred_element_type=F32)
    y_cf = jnp.dot(a_ref[:, cw:2 * cw], wcf_ref[...], preferred_element_type=F32)
    y_ret = jnp.dot(a_ref[:, 2 * cw:], wret_ref[...], preferred_element_type=F32)
    merged = (_sigmoid(gsc_ref[...].astype(F32)) * y_sc + _sigmoid(gcf_ref[...].astype(F32)) * y_cf
              + _sigmoid(gret_ref[...].astype(F32)) * y_ret).astype(BF16)

    @pl.when(first)
    def _():
        _fetch_rows_wait(x_hbm, o_ref, sems, row0)

    o_ref[...] += jnp.dot(merged, wmix_ref[...], preferred_element_type=F32)


def _merge(x, a, h, w_sc, w_cf, w_ret, w_mix, layer, *, gate_col0, bm, bn):
    m, d = x.shape
    cw = w_sc.shape[1]
    rw = w_ret.shape[1]
    g0 = gate_col0 // bn
    gstep = d // bn

    def gspec(kidx):
        return pl.BlockSpec((bm, bn), lambda i, j: (i, g0 + kidx * gstep + j))

    kern = functools.partial(_merge_kernel, cw=cw)
    return pl.pallas_call(
        kern,
        grid=(m // bm, d // bn),
        in_specs=[
            pl.BlockSpec(memory_space=pl.ANY),
            pl.BlockSpec((bm, 2 * cw + rw), lambda i, j: (i, 0)),
            gspec(0), gspec(1), gspec(2),
            pl.BlockSpec((None, cw, bn), lambda i, j: (layer, 0, j)),
            pl.BlockSpec((None, cw, bn), lambda i, j: (layer, 0, j)),
            pl.BlockSpec((None, rw, bn), lambda i, j: (layer, 0, j)),
            pl.BlockSpec((None, bn, d), lambda i, j: (layer, j, 0)),
        ],
        out_specs=pl.BlockSpec((bm, d), lambda i, j: (i, 0)),
        out_shape=jax.ShapeDtypeStruct((m, d), F32),
        scratch_shapes=[pltpu.SemaphoreType.DMA((bm // FETCH_ROWS,))],
        compiler_params=_params(),
        name="merge",
    )(x, a, h, h, h, w_sc, w_cf, w_ret, w_mix)


def _tile(n, pref):
    return pref if n % pref == 0 else n


def kernel(x, positions, norm_ffn1, ffn1_in, ffn1_out, norm_mix, w_in, sc_conv_w, cf_dw_w, cf_dw_b, cf_ln_w, cf_ln_b, ret_gn_w, w_sc_out, w_cf_out, w_ret_out, w_mix_out, norm_ffn2, ffn2_in, ffn2_out, norm_final):
    b, s, d = x.shape
    m = b * s
    depth = w_in.shape[0]
    cw = w_sc_out.shape[1]
    rw = w_ret_out.shape[1]
    hd = rw // RET_HEADS
    half = hd // 2
    assert w_cf_out.shape[1] == cw and rw % cw == 0 and half == LANES
    gate_col0 = 5 * cw + 4 * rw
    assert w_in.shape[2] == gate_col0 + 3 * d

    bm_ffn = _tile(m, 1024)
    bf = _tile(ffn1_out.shape[1], 256)
    bm_in = _tile(m, 1024)
    bn_in = _tile(w_in.shape[2], 1024)
    bm_mg = _tile(m, 512)
    bn_mg = _tile(d, 512)
    ts = _tile(s, 256)

    inv_freq = (ROPE_BASE ** (-jnp.arange(half, dtype=F32) / half)).reshape(1, half)
    posb = jnp.broadcast_to(positions.astype(F32)[..., None], (b, s, LANES))

    ffn1_in, ffn1_out, ffn2_in, ffn2_out, w_in, w_sc_out, w_cf_out, w_ret_out, w_mix_out = (
        w.astype(BF16) for w in
        (ffn1_in, ffn1_out, ffn2_in, ffn2_out, w_in, w_sc_out, w_cf_out, w_ret_out, w_mix_out))

    xf = x.reshape(m, d)
    for l in range(depth):
        xf = _ffn(xf, norm_ffn1[l], ffn1_in, ffn1_out, l, norm_final,
                  final_norm=False, bm=bm_ffn, bf=bf)
        h = _inproj(xf, norm_mix[l], w_in, l, bm=bm_in, bn=bn_in)
        a = _mixer(h.reshape(b, s, -1), posb, inv_freq, sc_conv_w[l], cf_dw_w[l],
                   cf_dw_b[l].reshape(1, cw), cf_ln_w[l].reshape(1, cw), cf_ln_b[l].reshape(1, cw),
                   ret_gn_w[l].reshape(1, rw), ts=ts, cw=cw, rw=rw)
        xf = _merge(xf, a.reshape(m, -1), h, w_sc_out, w_cf_out, w_ret_out, w_mix_out, l,
                    gate_col0=gate_col0, bm=bm_mg, bn=bn_mg)
        xf = _ffn(xf, norm_ffn2[l], ffn2_in, ffn2_out, l, norm_final,
                  final_norm=(l == depth - 1), bm=bm_ffn, bf=bf)
    return xf.reshape(b, s, d)
```

```python
import functools
import math

import jax
import jax.numpy as jnp
from jax import lax
from jax.experimental import pallas as pl
from jax.experimental.pallas import tpu as pltpu

EPS = 1e-6
ROPE_BASE = 10000.0
RET_HEADS = 8
RET_CHUNK = 128
SC_KERNEL = 3
CF_KERNEL = 31
SUBLANES = 8
LANES = 128
SC_HALO = 8
CF_HALO = 32
CONV_ROWS = 64
NORM_ROWS = 64
FETCH_ROWS = 128
VMEM_LIMIT = 58 * 1024 * 1024

F32 = jnp.float32
BF16 = jnp.bfloat16


def _sigmoid(x):
    return 1.0 / (1.0 + jnp.exp(-x))


def _silu(x):
    return x * _sigmoid(x)


def _rms_rows(x, gain):
    return (x * lax.rsqrt(jnp.mean(x * x, axis=-1, keepdims=True) + EPS)) * gain


def _row_copy(x_hbm, dst_ref, sems, row0, c):
    return pltpu.make_async_copy(
        x_hbm.at[pl.ds(row0 + c * FETCH_ROWS, FETCH_ROWS), :],
        dst_ref.at[pl.ds(c * FETCH_ROWS, FETCH_ROWS), :],
        sems.at[c])


def _fetch_rows_start(x_hbm, dst_ref, sems, row0):
    for c in range(dst_ref.shape[0] // FETCH_ROWS):
        _row_copy(x_hbm, dst_ref, sems, row0, c).start()


def _fetch_rows_wait(x_hbm, dst_ref, sems, row0):
    for c in range(dst_ref.shape[0] // FETCH_ROWS):
        _row_copy(x_hbm, dst_ref, sems, row0, c).wait()


def _fetch_and_norm(x_hbm, dst_ref, sems, row0, gain_ref, xn_ref):
    _fetch_rows_start(x_hbm, dst_ref, sems, row0)
    for c in range(dst_ref.shape[0] // FETCH_ROWS):
        _row_copy(x_hbm, dst_ref, sems, row0, c).wait()
        for r0 in range(c * FETCH_ROWS, (c + 1) * FETCH_ROWS, NORM_ROWS):
            rows = slice(r0, r0 + NORM_ROWS)
            xn_ref[rows, :] = _rms_rows(dst_ref[rows, :], gain_ref[...]).astype(BF16)


def _params():
    return pltpu.CompilerParams(
        dimension_semantics=("arbitrary", "arbitrary"), vmem_limit_bytes=VMEM_LIMIT)


def _ffn_kernel(x_hbm, gain_ref, wgu_ref, wo_ref, fgain_ref, o_ref, xn_ref, sems, *, final_norm):
    j = pl.program_id(1)
    row0 = pl.program_id(0) * o_ref.shape[0]
    bf = wo_ref.shape[0]

    @pl.when(j == 0)
    def _():
        _fetch_and_norm(x_hbm, o_ref, sems, row0, gain_ref, xn_ref)

    gu = jnp.dot(xn_ref[...], wgu_ref[...], preferred_element_type=F32)
    g, u = gu[:, :bf], gu[:, bf:]
    a = (0.5 * _silu(g) * u).astype(BF16)
    o_ref[...] += jnp.dot(a, wo_ref[...], preferred_element_type=F32)

    if final_norm:
        @pl.when(j == pl.num_programs(1) - 1)
        def _():
            def body(i, carry):
                r = pl.ds(pl.multiple_of(i * NORM_ROWS, NORM_ROWS), NORM_ROWS)
                o_ref[r, :] = _rms_rows(o_ref[r, :], fgain_ref[...])
                return carry
            lax.fori_loop(0, o_ref.shape[0] // NORM_ROWS, body, 0)


def _ffn(x, gain, w_in, w_out, layer, final_gain, *, final_norm, bm, bf):
    m, d = x.shape
    f = w_out.shape[1]
    kern = functools.partial(_ffn_kernel, final_norm=final_norm)
    return pl.pallas_call(
        kern,
        grid=(m // bm, f // bf),
        in_specs=[
            pl.BlockSpec(memory_space=pl.ANY),
            pl.BlockSpec((1, d), lambda i, j: (0, 0)),
            pl.BlockSpec((None, d, 2 * bf), lambda i, j: (layer, 0, j)),
            pl.BlockSpec((None, bf, d), lambda i, j: (layer, j, 0)),
            pl.BlockSpec((1, d), lambda i, j: (0, 0)),
        ],
        out_specs=pl.BlockSpec((bm, d), lambda i, j: (i, 0)),
        out_shape=jax.ShapeDtypeStruct((m, d), F32),
        scratch_shapes=[pltpu.VMEM((bm, d), BF16), pltpu.SemaphoreType.DMA((bm // FETCH_ROWS,))],
        compiler_params=_params(),
        name="ffn_final" if final_norm else "ffn",
    )(x, gain.reshape(1, d), w_in, w_out, final_gain.reshape(1, d))


def _inproj_kernel(x_hbm, gain_ref, w_ref, o_ref, x_ref, xn_ref, sems):
    row0 = pl.program_id(0) * x_ref.shape[0]

    @pl.when(pl.program_id(1) == 0)
    def _():
        _fetch_and_norm(x_hbm, x_ref, sems, row0, gain_ref, xn_ref)

    o_ref[...] = jnp.dot(xn_ref[...], w_ref[...], preferred_element_type=F32).astype(o_ref.dtype)


def _inproj(x, gain, w, layer, *, bm, bn):
    m, d = x.shape
    n = w.shape[2]
    return pl.pallas_call(
        _inproj_kernel,
        grid=(m // bm, n // bn),
        in_specs=[
            pl.BlockSpec(memory_space=pl.ANY),
            pl.BlockSpec((1, d), lambda i, j: (0, 0)),
            pl.BlockSpec((None, d, bn), lambda i, j: (layer, 0, j)),
        ],
        out_specs=pl.BlockSpec((bm, bn), lambda i, j: (i, j)),
        out_shape=jax.ShapeDtypeStruct((m, n), BF16),
        scratch_shapes=[pltpu.VMEM((bm, d), F32), pltpu.VMEM((bm, d), BF16),
                        pltpu.SemaphoreType.DMA((bm // FETCH_ROWS,))],
        compiler_params=_params(),
        name="inproj",
    )(x, gain.reshape(1, d), w)


def _spread_taps(w_ref, wb_ref):
    for j in range(w_ref.shape[0]):
        wb_ref[j * SUBLANES:(j + 1) * SUBLANES, :] = jnp.broadcast_to(
            w_ref[j:j + 1, :], (SUBLANES, w_ref.shape[1]))


def _causal_conv_block(buf, wb_ref, r0, cs, halo, ktaps):
    span = -(-(ktaps - 1) // SUBLANES) * SUBLANES
    base = halo + r0 - span
    win = buf[base:base + span + CONV_ROWS, cs]
    lead = span - (ktaps - 1)
    groups = CONV_ROWS // SUBLANES
    acc = None
    for phase in range(SUBLANES):
        taps = [j for j in range(ktaps) if (lead + j) % SUBLANES == phase]
        if not taps:
            continue
        shifted = win[phase:phase + span + CONV_ROWS - SUBLANES] if phase else win
        for j in taps:
            a0 = lead + j - phase
            w = wb_ref[j * SUBLANES:(j + 1) * SUBLANES, cs]
            term = shifted[a0:a0 + CONV_ROWS].reshape(groups, SUBLANES, LANES) * w[None]
            acc = term if acc is None else acc + term
    return acc.reshape(CONV_ROWS, LANES)


def _mixer_kernel(*refs, ts, cw, hd, nq):
    (scb_ref, scc_ref, sch_ref, cfv_ref, cfg_ref) = refs[:5]
    q_refs = refs[5:5 + nq]
    k_refs = refs[5 + nq:5 + 2 * nq]
    v_refs = refs[5 + 2 * nq:5 + 3 * nq]
    g_refs = refs[5 + 3 * nq:5 + 4 * nq]
    (pos_ref, invf_ref, scw_ref, cfw_ref, cfb_ref, lnw_ref, lnb_ref, gnw_ref,
     o_ref, sbuf, cbuf, state, scwb, cfwb) = refs[5 + 4 * nq:]
    half = hd // 2
    heads_per_ref = cw // hd

    @pl.when(pl.program_id(1) == 0)
    def _():
        sbuf[0:SC_HALO, :] = jnp.zeros((SC_HALO, cw), F32)
        cbuf[0:CF_HALO, :] = jnp.zeros((CF_HALO, cw), F32)
        state[...] = jnp.zeros(state.shape, F32)
        _spread_taps(scw_ref, scwb)
        _spread_taps(cfw_ref, cfwb)

    sbuf[SC_HALO:SC_HALO + ts, :] = scc_ref[...].astype(F32) * sch_ref[...].astype(F32)
    for r0 in range(0, ts, CONV_ROWS):
        rows = slice(r0, r0 + CONV_ROWS)
        for c0 in range(0, cw, LANES):
            cs = slice(c0, c0 + LANES)
            y = _causal_conv_block(sbuf, scwb, r0, cs, SC_HALO, SC_KERNEL)
            o_ref[rows, cs] = (scb_ref[rows, cs].astype(F32) * y).astype(o_ref.dtype)
    sbuf[0:SC_HALO, :] = sbuf[ts:ts + SC_HALO, :]

    cbuf[CF_HALO:CF_HALO + ts, :] = cfv_ref[...].astype(F32) * _sigmoid(cfg_ref[...].astype(F32))
    for r0 in range(0, ts, CONV_ROWS):
        zs = []
        for c0 in range(0, cw, LANES):
            cs = slice(c0, c0 + LANES)
            zs.append(_causal_conv_block(cbuf, cfwb, r0, cs, CF_HALO, CF_KERNEL) + cfb_ref[:, cs])
        z = jnp.concatenate(zs, axis=-1)
        mu = jnp.mean(z, axis=-1, keepdims=True)
        zc = z - mu
        var = jnp.mean(zc * zc, axis=-1, keepdims=True)
        y = zc * lax.rsqrt(var + EPS) * lnw_ref[...] + lnb_ref[...]
        o_ref[r0:r0 + CONV_ROWS, cw:2 * cw] = _silu(y).astype(o_ref.dtype)
    cbuf[0:CF_HALO, :] = cbuf[ts:ts + CF_HALO, :]

    ang = pos_ref[...] * invf_ref[...]
    cos = jnp.cos(ang)
    sin = jnp.sin(ang)
    c = RET_CHUNK
    row = lax.broadcasted_iota(jnp.int32, (c, c), 0).astype(F32)
    col = lax.broadcasted_iota(jnp.int32, (c, c), 1).astype(F32)
    rel = row - col
    rowd = lax.broadcasted_iota(jnp.int32, (c, hd), 0).astype(F32)
    k_scale = float(hd) ** -0.5

    def rot(t, cs_, sn_):
        t1, t2 = t[:, :half], t[:, half:]
        return jnp.concatenate([t1 * cs_ - t2 * sn_, t1 * sn_ + t2 * cs_], axis=-1)

    for h in range(RET_HEADS):
        lg = math.log(1.0 - 2.0 ** (-5.0 - h))
        mask = jnp.exp(jnp.where(rel >= 0, rel * lg, -jnp.inf))
        qd = jnp.exp((rowd + 1.0) * lg)
        kd = jnp.exp((c - 1.0 - rowd) * lg)
        cd = math.exp(c * lg)
        ri, hl = divmod(h, heads_per_ref)
        hs = slice(hl * hd, (hl + 1) * hd)
        for ci in range(ts // c):
            rs = slice(ci * c, (ci + 1) * c)
            cs_, sn_ = cos[rs], sin[rs]
            q = rot(q_refs[ri][rs, hs].astype(F32), cs_, sn_)
            k = rot(k_refs[ri][rs, hs].astype(F32), cs_, sn_) * k_scale
            v = v_refs[ri][rs, hs]
            qb = q.astype(BF16)
            st = state[h]
            scores = lax.dot_general(qb, k.astype(BF16), (((1,), (1,)), ((), ())),
                                     preferred_element_type=F32) * mask
            intra = jnp.dot(scores.astype(BF16), v, preferred_element_type=F32)
            inter = jnp.dot(qb, st.astype(BF16), preferred_element_type=F32) * qd
            state[h] = st * cd + lax.dot_general((k * kd).astype(BF16), v, (((0,), (0,)), ((), ())),
                                                 preferred_element_type=F32)
            o = intra + inter
            mu = jnp.mean(o, axis=-1, keepdims=True)
            oc = o - mu
            var = jnp.mean(oc * oc, axis=-1, keepdims=True)
            on = oc * lax.rsqrt(var + EPS) * gnw_ref[:, h * hd:(h + 1) * hd]
            gate = g_refs[ri][rs, hs].astype(F32)
            o_ref[rs, 2 * cw + h * hd:2 * cw + (h + 1) * hd] = (_silu(gate) * on).astype(o_ref.dtype)


def _mixer(h3, posb, invf, sc_w, cf_w, cf_b, ln_w, ln_b, gn_w, *, col0, ts, cw, rw):
    b, s, n = h3.shape
    hd = rw // RET_HEADS
    nq = rw // cw
    ncol = 5 + 4 * nq
    cb0 = col0 // cw

    def hspec(col):
        return pl.BlockSpec((None, ts, cw), lambda bi, si: (bi, si, cb0 + col))

    def full(a):
        return pl.BlockSpec(a.shape, lambda bi, si: (0,) * a.ndim)

    small = [invf, sc_w, cf_w, cf_b, ln_w, ln_b, gn_w]
    kern = functools.partial(_mixer_kernel, ts=ts, cw=cw, hd=hd, nq=nq)
    return pl.pallas_call(
        kern,
        grid=(b, s // ts),
        in_specs=[hspec(cidx) for cidx in range(ncol)]
        + [pl.BlockSpec((None, ts, LANES), lambda bi, si: (bi, si, 0))]
        + [full(a) for a in small],
        out_specs=pl.BlockSpec((None, ts, 2 * cw + rw), lambda bi, si: (bi, si, 0)),
        out_shape=jax.ShapeDtypeStruct((b, s, 2 * cw + rw), BF16),
        scratch_shapes=[
            pltpu.VMEM((SC_HALO + ts, cw), F32),
            pltpu.VMEM((CF_HALO + ts, cw), F32),
            pltpu.VMEM((RET_HEADS, hd, hd), F32),
            pltpu.VMEM((SC_KERNEL * SUBLANES, cw), F32),
            pltpu.VMEM((CF_KERNEL * SUBLANES, cw), F32),
        ],
        compiler_params=_params(),
        name="mixer",
    )(*([h3] * ncol), posb, *small)


def _merge_kernel(a_ref, g_ref, wbr_ref, wmix_ref, x_ref, o_ref, mg_ref, *, cw):
    j = pl.program_id(1)
    nt, _, bn = mg_ref.shape

    @pl.when(j < nt)
    def _():
        y_sc = jnp.dot(a_ref[:, 0:cw], wbr_ref[0:cw, :], preferred_element_type=F32)
        y_cf = jnp.dot(a_ref[:, cw:2 * cw], wbr_ref[cw:2 * cw, :], preferred_element_type=F32)
        y_ret = jnp.dot(a_ref[:, 2 * cw:], wbr_ref[2 * cw:, :], preferred_element_type=F32)
        merged = (_sigmoid(g_ref[:, 0:bn].astype(F32)) * y_sc
                  + _sigmoid(g_ref[:, bn:2 * bn].astype(F32)) * y_cf
                  + _sigmoid(g_ref[:, 2 * bn:].astype(F32)) * y_ret)
        mg_ref[j] = merged.astype(BF16)

    @pl.when(j >= nt)
    def _():
        acc = x_ref[...]
        for c in range(nt):
            acc = acc + jnp.dot(mg_ref[c], wmix_ref[c * bn:(c + 1) * bn, :], preferred_element_type=F32)
        o_ref[...] = acc


def _merge(x, a, h, w_branch, w_mix, layer, *, cw, bm, bn):
    m, d = x.shape
    kb = w_branch.shape[1]
    nt = d // bn

    def build(j):
        return jnp.minimum(j, nt - 1)

    def emit(j):
        return jnp.maximum(j - nt, 0)

    kern = functools.partial(_merge_kernel, cw=cw)
    return pl.pallas_call(
        kern,
        grid=(m // bm, 2 * nt),
        in_specs=[
            pl.BlockSpec((bm, kb), lambda i, j: (i, 0)),
            pl.BlockSpec((bm, 3 * bn), lambda i, j: (i, build(j))),
            pl.BlockSpec((None, kb, bn), lambda i, j: (layer, 0, build(j))),
            pl.BlockSpec((None, d, bn), lambda i, j: (layer, 0, emit(j))),
            pl.BlockSpec((bm, bn), lambda i, j: (i, emit(j))),
        ],
        out_specs=pl.BlockSpec((bm, bn), lambda i, j: (i, emit(j))),
        out_shape=jax.ShapeDtypeStruct((m, d), F32),
        scratch_shapes=[pltpu.VMEM((nt, bm, bn), BF16)],
        compiler_params=_params(),
        name="merge",
    )(a, h, w_branch, w_mix, x)


def _tile(n, pref):
    return pref if n % pref == 0 else n


def _interleave_cols(w, parts, blk):
    lead = w.shape[:-1]
    n = w.shape[-1] // parts
    return w.reshape(*lead, parts, n // blk, blk).swapaxes(-3, -2).reshape(*lead, parts * n)


def kernel(x, positions, norm_ffn1, ffn1_in, ffn1_out, norm_mix, w_in, sc_conv_w, cf_dw_w, cf_dw_b, cf_ln_w, cf_ln_b, ret_gn_w, w_sc_out, w_cf_out, w_ret_out, w_mix_out, norm_ffn2, ffn2_in, ffn2_out, norm_final):
    b, s, d = x.shape
    m = b * s
    depth = w_in.shape[0]
    cw = w_sc_out.shape[1]
    rw = w_ret_out.shape[1]
    hd = rw // RET_HEADS
    half = hd // 2
    assert w_cf_out.shape[1] == cw and rw % cw == 0 and half == LANES
    gate_col0 = 5 * cw + 4 * rw
    assert w_in.shape[2] == gate_col0 + 3 * d

    bm_ffn = _tile(m, 1024)
    bf = _tile(ffn1_out.shape[1], 256)
    bm_in = _tile(m, 1024)
    bn_in = _tile(w_in.shape[2], 1024)
    bm_mg = _tile(m, 1024)
    bn_mg = _tile(d, 512)
    ts = _tile(s, 256)

    inv_freq = (ROPE_BASE ** (-jnp.arange(half, dtype=F32) / half)).reshape(1, half)
    posb = jnp.broadcast_to(positions.astype(F32)[..., None], (b, s, LANES))

    ffn1_in = _interleave_cols(ffn1_in.astype(BF16), 2, bf)
    ffn2_in = _interleave_cols(ffn2_in.astype(BF16), 2, bf)
    ffn1_out, ffn2_out, w_mix_out = (w.astype(BF16) for w in (ffn1_out, ffn2_out, w_mix_out))
    w_in = w_in.astype(BF16)
    w_in = jnp.concatenate(
        [_interleave_cols(w_in[..., gate_col0:], 3, bn_mg), w_in[..., :gate_col0]], axis=-1)
    w_branch = jnp.concatenate([w_sc_out, w_cf_out, w_ret_out], axis=1).astype(BF16)

    xf = x.reshape(m, d)
    for l in range(depth):
        xf = _ffn(xf, norm_ffn1[l], ffn1_in, ffn1_out, l, norm_final,
                  final_norm=False, bm=bm_ffn, bf=bf)
        h = _inproj(xf, norm_mix[l], w_in, l, bm=bm_in, bn=bn_in)
        a = _mixer(h.reshape(b, s, -1), posb, inv_freq, sc_conv_w[l], cf_dw_w[l],
                   cf_dw_b[l].reshape(1, cw), cf_ln_w[l].reshape(1, cw), cf_ln_b[l].reshape(1, cw),
                   ret_gn_w[l].reshape(1, rw), col0=3 * d, ts=ts, cw=cw, rw=rw)
        xf = _merge(xf, a.reshape(m, -1), h, w_branch, w_mix_out, l, cw=cw, bm=bm_mg, bn=bn_mg)
        xf = _ffn(xf, norm_ffn2[l], ffn2_in, ffn2_out, l, norm_final,
                  final_norm=(l == depth - 1), bm=bm_ffn, bf=bf)
    return xf.reshape(b, s, d)
```

```python
import functools
import math

import jax
import jax.numpy as jnp
from jax import lax
from jax.experimental import pallas as pl
from jax.experimental.pallas import tpu as pltpu

EPS = 1e-6
ROPE_BASE = 10000.0
RET_HEADS = 8
RET_CHUNK = 128
SC_KERNEL = 3
CF_KERNEL = 31
SUBLANES = 8
LANES = 128
SC_HALO = 8
CF_HALO = 32
CONV_ROWS = 64
NORM_ROWS = 64
FETCH_ROWS = 128
VMEM_LIMIT = 58 * 1024 * 1024

F32 = jnp.float32
BF16 = jnp.bfloat16


def _sigmoid(x):
    return 1.0 / (1.0 + jnp.exp(-x))


def _silu(x):
    return x * _sigmoid(x)


def _rms_rows(x, gain):
    return (x * lax.rsqrt(jnp.mean(x * x, axis=-1, keepdims=True) + EPS)) * gain


def _row_copy(x_hbm, dst_ref, sems, row0, c):
    return pltpu.make_async_copy(
        x_hbm.at[pl.ds(row0 + c * FETCH_ROWS, FETCH_ROWS), :],
        dst_ref.at[pl.ds(c * FETCH_ROWS, FETCH_ROWS), :],
        sems.at[c])


def _fetch_rows_start(x_hbm, dst_ref, sems, row0):
    for c in range(dst_ref.shape[0] // FETCH_ROWS):
        _row_copy(x_hbm, dst_ref, sems, row0, c).start()


def _fetch_and_norm(x_hbm, dst_ref, sems, row0, gain_ref, xn_ref):
    _fetch_rows_start(x_hbm, dst_ref, sems, row0)
    for c in range(dst_ref.shape[0] // FETCH_ROWS):
        _row_copy(x_hbm, dst_ref, sems, row0, c).wait()
        for r0 in range(c * FETCH_ROWS, (c + 1) * FETCH_ROWS, NORM_ROWS):
            rows = slice(r0, r0 + NORM_ROWS)
            xn_ref[rows, :] = _rms_rows(dst_ref[rows, :], gain_ref[...]).astype(BF16)


def _params():
    return pltpu.CompilerParams(
        dimension_semantics=("arbitrary", "arbitrary"), vmem_limit_bytes=VMEM_LIMIT)


def _ffn_kernel(x_hbm, gain_ref, wg_ref, wu_ref, wo_ref, fgain_ref, o_ref, xn_ref, sems, *, final_norm):
    j = pl.program_id(1)
    row0 = pl.program_id(0) * o_ref.shape[0]

    @pl.when(j == 0)
    def _():
        _fetch_and_norm(x_hbm, o_ref, sems, row0, gain_ref, xn_ref)

    xn = xn_ref[...]
    g = jnp.dot(xn, wg_ref[...], preferred_element_type=F32)
    u = jnp.dot(xn, wu_ref[...], preferred_element_type=F32)
    a = (0.5 * _silu(g) * u).astype(BF16)
    o_ref[...] += jnp.dot(a, wo_ref[...], preferred_element_type=F32)

    if final_norm:
        @pl.when(j == pl.num_programs(1) - 1)
        def _():
            def body(i, carry):
                r = pl.ds(pl.multiple_of(i * NORM_ROWS, NORM_ROWS), NORM_ROWS)
                o_ref[r, :] = _rms_rows(o_ref[r, :], fgain_ref[...])
                return carry
            lax.fori_loop(0, o_ref.shape[0] // NORM_ROWS, body, 0)


def _ffn(x, gain, w_in, w_out, layer, final_gain, *, final_norm, bm, bf):
    m, d = x.shape
    f = w_out.shape[1]
    nf = f // bf
    kern = functools.partial(_ffn_kernel, final_norm=final_norm)
    return pl.pallas_call(
        kern,
        grid=(m // bm, nf),
        in_specs=[
            pl.BlockSpec(memory_space=pl.ANY),
            pl.BlockSpec((1, d), lambda i, j: (0, 0)),
            pl.BlockSpec((None, d, bf), lambda i, j: (layer, 0, j)),
            pl.BlockSpec((None, d, bf), lambda i, j: (layer, 0, j + nf)),
            pl.BlockSpec((None, bf, d), lambda i, j: (layer, j, 0)),
            pl.BlockSpec((1, d), lambda i, j: (0, 0)),
        ],
        out_specs=pl.BlockSpec((bm, d), lambda i, j: (i, 0)),
        out_shape=jax.ShapeDtypeStruct((m, d), F32),
        scratch_shapes=[pltpu.VMEM((bm, d), BF16), pltpu.SemaphoreType.DMA((bm // FETCH_ROWS,))],
        compiler_params=_params(),
        name="ffn_final" if final_norm else "ffn",
    )(x, gain.reshape(1, d), w_in, w_in, w_out, final_gain.reshape(1, d))


def _inproj_kernel(x_hbm, gain_ref, w_ref, o_ref, x_ref, xn_ref, sems):
    row0 = pl.program_id(0) * x_ref.shape[0]

    @pl.when(pl.program_id(1) == 0)
    def _():
        _fetch_and_norm(x_hbm, x_ref, sems, row0, gain_ref, xn_ref)

    o_ref[...] = jnp.dot(xn_ref[...], w_ref[...], preferred_element_type=F32).astype(o_ref.dtype)


def _inproj(x, gain, w, layer, *, bm, bn):
    m, d = x.shape
    n = w.shape[2]
    return pl.pallas_call(
        _inproj_kernel,
        grid=(m // bm, n // bn),
        in_specs=[
            pl.BlockSpec(memory_space=pl.ANY),
            pl.BlockSpec((1, d), lambda i, j: (0, 0)),
            pl.BlockSpec((None, d, bn), lambda i, j: (layer, 0, j)),
        ],
        out_specs=pl.BlockSpec((bm, bn), lambda i, j: (i, j)),
        out_shape=jax.ShapeDtypeStruct((m, n), BF16),
        scratch_shapes=[pltpu.VMEM((bm, d), F32), pltpu.VMEM((bm, d), BF16),
                        pltpu.SemaphoreType.DMA((bm // FETCH_ROWS,))],
        compiler_params=_params(),
        name="inproj",
    )(x, gain.reshape(1, d), w)


def _spread_taps(w_ref, wb_ref):
    for j in range(w_ref.shape[0]):
        wb_ref[j * SUBLANES:(j + 1) * SUBLANES, :] = jnp.broadcast_to(
            w_ref[j:j + 1, :], (SUBLANES, w_ref.shape[1]))


def _causal_conv_block(buf, wb_ref, r0, cs, halo, ktaps):
    span = -(-(ktaps - 1) // SUBLANES) * SUBLANES
    base = halo + r0 - span
    win = buf[base:base + span + CONV_ROWS, cs]
    lead = span - (ktaps - 1)
    groups = CONV_ROWS // SUBLANES
    acc = None
    for phase in range(SUBLANES):
        taps = [j for j in range(ktaps) if (lead + j) % SUBLANES == phase]
        if not taps:
            continue
        shifted = win[phase:phase + span + CONV_ROWS - SUBLANES] if phase else win
        for j in taps:
            a0 = lead + j - phase
            w = wb_ref[j * SUBLANES:(j + 1) * SUBLANES, cs]
            term = shifted[a0:a0 + CONV_ROWS].reshape(groups, SUBLANES, LANES) * w[None]
            acc = term if acc is None else acc + term
    return acc.reshape(CONV_ROWS, LANES)


def _mixer_kernel(*refs, ts, cw, hd, nq):
    (scb_ref, scc_ref, sch_ref, cfv_ref, cfg_ref) = refs[:5]
    q_refs = refs[5:5 + nq]
    k_refs = refs[5 + nq:5 + 2 * nq]
    v_refs = refs[5 + 2 * nq:5 + 3 * nq]
    g_refs = refs[5 + 3 * nq:5 + 4 * nq]
    (pos_ref, invf_ref, scw_ref, cfw_ref, cfb_ref, lnw_ref, lnb_ref, gnw_ref,
     o_ref, sbuf, cbuf, state, scwb, cfwb) = refs[5 + 4 * nq:]
    half = hd // 2
    heads_per_ref = cw // hd

    @pl.when(pl.program_id(1) == 0)
    def _():
        sbuf[0:SC_HALO, :] = jnp.zeros((SC_HALO, cw), F32)
        cbuf[0:CF_HALO, :] = jnp.zeros((CF_HALO, cw), F32)
        state[...] = jnp.zeros(state.shape, F32)
        _spread_taps(scw_ref, scwb)
        _spread_taps(cfw_ref, cfwb)

    sbuf[SC_HALO:SC_HALO + ts, :] = scc_ref[...].astype(F32) * sch_ref[...].astype(F32)
    for r0 in range(0, ts, CONV_ROWS):
        rows = slice(r0, r0 + CONV_ROWS)
        for c0 in range(0, cw, LANES):
            cs = slice(c0, c0 + LANES)
            y = _causal_conv_block(sbuf, scwb, r0, cs, SC_HALO, SC_KERNEL)
            o_ref[rows, cs] = (scb_ref[rows, cs].astype(F32) * y).astype(o_ref.dtype)
    sbuf[0:SC_HALO, :] = sbuf[ts:ts + SC_HALO, :]

    cbuf[CF_HALO:CF_HALO + ts, :] = cfv_ref[...].astype(F32) * _sigmoid(cfg_ref[...].astype(F32))
    for r0 in range(0, ts, CONV_ROWS):
        zs = []
        for c0 in range(0, cw, LANES):
            cs = slice(c0, c0 + LANES)
            zs.append(_causal_conv_block(cbuf, cfwb, r0, cs, CF_HALO, CF_KERNEL) + cfb_ref[:, cs])
        z = jnp.concatenate(zs, axis=-1)
        mu = jnp.mean(z, axis=-1, keepdims=True)
        zc = z - mu
        var = jnp.mean(zc * zc, axis=-1, keepdims=True)
        y = zc * lax.rsqrt(var + EPS) * lnw_ref[...] + lnb_ref[...]
        o_ref[r0:r0 + CONV_ROWS, cw:2 * cw] = _silu(y).astype(o_ref.dtype)
    cbuf[0:CF_HALO, :] = cbuf[ts:ts + CF_HALO, :]

    ang = pos_ref[...] * invf_ref[...]
    cos = jnp.cos(ang)
    sin = jnp.sin(ang)
    c = RET_CHUNK
    row = lax.broadcasted_iota(jnp.int32, (c, c), 0).astype(F32)
    col = lax.broadcasted_iota(jnp.int32, (c, c), 1).astype(F32)
    rel = row - col
    rowd = lax.broadcasted_iota(jnp.int32, (c, hd), 0).astype(F32)
    k_scale = float(hd) ** -0.5

    def rot(t, cs_, sn_):
        t1, t2 = t[:, :half], t[:, half:]
        return jnp.concatenate([t1 * cs_ - t2 * sn_, t1 * sn_ + t2 * cs_], axis=-1)

    for h in range(RET_HEADS):
        lg = math.log(1.0 - 2.0 ** (-5.0 - h))
        mask = jnp.exp(jnp.where(rel >= 0, rel * lg, -jnp.inf))
        qd = jnp.exp((rowd + 1.0) * lg)
        kd = jnp.exp((c - 1.0 - rowd) * lg)
        cd = math.exp(c * lg)
        ri, hl = divmod(h, heads_per_ref)
        hs = slice(hl * hd, (hl + 1) * hd)
        for ci in range(ts // c):
            rs = slice(ci * c, (ci + 1) * c)
            cs_, sn_ = cos[rs], sin[rs]
            q = rot(q_refs[ri][rs, hs].astype(F32), cs_, sn_)
            k = rot(k_refs[ri][rs, hs].astype(F32), cs_, sn_) * k_scale
            v = v_refs[ri][rs, hs]
            qb = q.astype(BF16)
            st = state[h]
            scores = lax.dot_general(qb, k.astype(BF16), (((1,), (1,)), ((), ())),
                                     preferred_element_type=F32) * mask
            intra = jnp.dot(scores.astype(BF16), v, preferred_element_type=F32)
            inter = jnp.dot(qb, st.astype(BF16), preferred_element_type=F32) * qd
            state[h] = st * cd + lax.dot_general((k * kd).astype(BF16), v, (((0,), (0,)), ((), ())),
                                                 preferred_element_type=F32)
            o = intra + inter
            mu = jnp.mean(o, axis=-1, keepdims=True)
            oc = o - mu
            var = jnp.mean(oc * oc, axis=-1, keepdims=True)
            on = oc * lax.rsqrt(var + EPS) * gnw_ref[:, h * hd:(h + 1) * hd]
            gate = g_refs[ri][rs, hs].astype(F32)
            o_ref[rs, 2 * cw + h * hd:2 * cw + (h + 1) * hd] = (_silu(gate) * on).astype(o_ref.dtype)


def _mixer(h3, posb, invf, sc_w, cf_w, cf_b, ln_w, ln_b, gn_w, *, col0, ts, cw, rw):
    b, s, n = h3.shape
    hd = rw // RET_HEADS
    nq = rw // cw
    ncol = 5 + 4 * nq
    cb0 = col0 // cw

    def hspec(col):
        return pl.BlockSpec((None, ts, cw), lambda bi, si: (bi, si, cb0 + col))

    def full(a):
        return pl.BlockSpec(a.shape, lambda bi, si: (0,) * a.ndim)

    small = [invf, sc_w, cf_w, cf_b, ln_w, ln_b, gn_w]
    kern = functools.partial(_mixer_kernel, ts=ts, cw=cw, hd=hd, nq=nq)
    return pl.pallas_call(
        kern,
        grid=(b, s // ts),
        in_specs=[hspec(cidx) for cidx in range(ncol)]
        + [pl.BlockSpec((None, ts, LANES), lambda bi, si: (bi, si, 0))]
        + [full(a) for a in small],
        out_specs=pl.BlockSpec((None, ts, 2 * cw + rw), lambda bi, si: (bi, si, 0)),
        out_shape=jax.ShapeDtypeStruct((b, s, 2 * cw + rw), BF16),
        scratch_shapes=[
            pltpu.VMEM((SC_HALO + ts, cw), F32),
            pltpu.VMEM((CF_HALO + ts, cw), F32),
            pltpu.VMEM((RET_HEADS, hd, hd), F32),
            pltpu.VMEM((SC_KERNEL * SUBLANES, cw), F32),
            pltpu.VMEM((CF_KERNEL * SUBLANES, cw), F32),
        ],
        compiler_params=_params(),
        name="mixer",
    )(*([h3] * ncol), posb, *small)


def _merge_kernel(a_ref, gsc_ref, gcf_ref, gret_ref, wsc_ref, wcf_ref, wret_ref, wmix_ref, x_ref,
                  o_ref, mg_ref, *, cw):
    j = pl.program_id(1)
    nt, _, bn = mg_ref.shape

    @pl.when(j < nt)
    def _():
        y_sc = jnp.dot(a_ref[:, 0:cw], wsc_ref[...], preferred_element_type=F32)
        y_cf = jnp.dot(a_ref[:, cw:2 * cw], wcf_ref[...], preferred_element_type=F32)
        y_ret = jnp.dot(a_ref[:, 2 * cw:], wret_ref[...], preferred_element_type=F32)
        merged = (_sigmoid(gsc_ref[...].astype(F32)) * y_sc + _sigmoid(gcf_ref[...].astype(F32)) * y_cf
                  + _sigmoid(gret_ref[...].astype(F32)) * y_ret)
        mg_ref[j] = merged.astype(BF16)

    @pl.when(j >= nt)
    def _():
        acc = x_ref[...]
        for c in range(nt):
            acc = acc + jnp.dot(mg_ref[c], wmix_ref[c * bn:(c + 1) * bn, :], preferred_element_type=F32)
        o_ref[...] = acc


def _merge(x, a, h, w_sc, w_cf, w_ret, w_mix, layer, *, gate_col0, bm, bn):
    m, d = x.shape
    cw = w_sc.shape[1]
    rw = w_ret.shape[1]
    nt = d // bn
    g0 = gate_col0 // bn

    def build(j):
        return jnp.minimum(j, nt - 1)

    def emit(j):
        return jnp.maximum(j - nt, 0)

    def gspec(kidx):
        return pl.BlockSpec((bm, bn), lambda i, j: (i, g0 + kidx * nt + build(j)))

    def wspec(k):
        return pl.BlockSpec((None, k, bn), lambda i, j: (layer, 0, build(j)))

    kern = functools.partial(_merge_kernel, cw=cw)
    return pl.pallas_call(
        kern,
        grid=(m // bm, 2 * nt),
        in_specs=[
            pl.BlockSpec((bm, 2 * cw + rw), lambda i, j: (i, 0)),
            gspec(0), gspec(1), gspec(2),
            wspec(cw), wspec(cw), wspec(rw),
            pl.BlockSpec((None, d, bn), lambda i, j: (layer, 0, emit(j))),
            pl.BlockSpec((bm, bn), lambda i, j: (i, emit(j))),
        ],
        out_specs=pl.BlockSpec((bm, bn), lambda i, j: (i, emit(j))),
        out_shape=jax.ShapeDtypeStruct((m, d), F32),
        scratch_shapes=[pltpu.VMEM((nt, bm, bn), BF16)],
        compiler_params=_params(),
        name="merge",
    )(a, h, h, h, w_sc, w_cf, w_ret, w_mix, x)


def _tile(n, pref):
    return pref if n % pref == 0 else n


def kernel(x, positions, norm_ffn1, ffn1_in, ffn1_out, norm_mix, w_in, sc_conv_w, cf_dw_w, cf_dw_b, cf_ln_w, cf_ln_b, ret_gn_w, w_sc_out, w_cf_out, w_ret_out, w_mix_out, norm_ffn2, ffn2_in, ffn2_out, norm_final):
    b, s, d = x.shape
    m = b * s
    depth = w_in.shape[0]
    cw = w_sc_out.shape[1]
    rw = w_ret_out.shape[1]
    hd = rw // RET_HEADS
    half = hd // 2
    assert w_cf_out.shape[1] == cw and rw % cw == 0 and half == LANES
    gate_col0 = 5 * cw + 4 * rw
    assert w_in.shape[2] == gate_col0 + 3 * d

    bm_ffn = _tile(m, 1024)
    bf = _tile(ffn1_out.shape[1], 256)
    bm_in = _tile(m, 1024)
    bn_in = _tile(w_in.shape[2], 1024)
    bm_mg = _tile(m, 1024)
    bn_mg = _tile(d, 512)
    ts = _tile(s, 256)

    inv_freq = (ROPE_BASE ** (-jnp.arange(half, dtype=F32) / half)).reshape(1, half)
    posb = jnp.broadcast_to(positions.astype(F32)[..., None], (b, s, LANES))

    ffn1_in, ffn1_out, ffn2_in, ffn2_out, w_in, w_sc_out, w_cf_out, w_ret_out, w_mix_out = (
        w.astype(BF16) for w in
        (ffn1_in, ffn1_out, ffn2_in, ffn2_out, w_in, w_sc_out, w_cf_out, w_ret_out, w_mix_out))

    xf = x.reshape(m, d)
    for l in range(depth):
        xf = _ffn(xf, norm_ffn1[l], ffn1_in, ffn1_out, l, norm_final,
                  final_norm=False, bm=bm_ffn, bf=bf)
        h = _inproj(xf, norm_mix[l], w_in, l, bm=bm_in, bn=bn_in)
        a = _mixer(h.reshape(b, s, -1), posb, inv_freq, sc_conv_w[l], cf_dw_w[l],
                   cf_dw_b[l].reshape(1, cw), cf_ln_w[l].reshape(1, cw), cf_ln_b[l].reshape(1, cw),
                   ret_gn_w[l].reshape(1, rw), col0=0, ts=ts, cw=cw, rw=rw)
        xf = _merge(xf, a.reshape(m, -1), h, w_sc_out, w_cf_out, w_ret_out, w_mix_out, l,
                    gate_col0=gate_col0, bm=bm_mg, bn=bn_mg)
        xf = _ffn(xf, norm_ffn2[l], ffn2_in, ffn2_out, l, norm_final,
                  final_norm=(l == depth - 1), bm=bm_ffn, bf=bf)
    return xf.reshape(b, s, d)
```

```python
import functools
import math

import jax
import jax.numpy as jnp
from jax import lax
from jax.experimental import pallas as pl
from jax.experimental.pallas import tpu as pltpu

EPS = 1e-6
ROPE_BASE = 10000.0
RET_HEADS = 8
RET_CHUNK = 128
SC_KERNEL = 3
CF_KERNEL = 31
SUBLANES = 8
LANES = 128
BF16_ROWS = 16
SC_HALO = 8
CF_HALO = 32
CONV_ROWS = 64
NORM_ROWS = 64
FETCH_ROWS = 128
VMEM_LIMIT = 58 * 1024 * 1024

F32 = jnp.float32
BF16 = jnp.bfloat16


def _cdiv(a, b):
    return -(-a // b)


def _sigmoid(x):
    return 1.0 / (1.0 + jnp.exp(-x))


def _silu(x):
    return x * _sigmoid(x)


def _rms_rows(x, gain):
    return (x * lax.rsqrt(jnp.mean(x * x, axis=-1, keepdims=True) + EPS)) * gain


def _row_copy(x_hbm, dst_ref, sems, row0, c):
    return pltpu.make_async_copy(
        x_hbm.at[pl.ds(row0 + c * FETCH_ROWS, FETCH_ROWS), :],
        dst_ref.at[pl.ds(c * FETCH_ROWS, FETCH_ROWS), :],
        sems.at[c])


def _fetch_rows_start(x_hbm, dst_ref, sems, row0):
    for c in range(dst_ref.shape[0] // FETCH_ROWS):
        _row_copy(x_hbm, dst_ref, sems, row0, c).start()


def _fetch_and_norm(x_hbm, dst_ref, sems, row0, gain_ref, xn_ref):
    _fetch_rows_start(x_hbm, dst_ref, sems, row0)
    for c in range(dst_ref.shape[0] // FETCH_ROWS):
        _row_copy(x_hbm, dst_ref, sems, row0, c).wait()
        for r0 in range(c * FETCH_ROWS, (c + 1) * FETCH_ROWS, NORM_ROWS):
            rows = slice(r0, r0 + NORM_ROWS)
            xn_ref[rows, :] = _rms_rows(dst_ref[rows, :], gain_ref[...]).astype(BF16)


def _stage_copy(x_hbm, stage_ref, sems, row0, c):
    slot = c % 2
    return pltpu.make_async_copy(
        x_hbm.at[pl.ds(row0 + c * FETCH_ROWS, FETCH_ROWS), :], stage_ref.at[slot], sems.at[slot])


def _stream_and_norm(x_hbm, stage_ref, sems, row0, gain_ref, xn_ref):
    n_chunks = xn_ref.shape[0] // FETCH_ROWS
    for c in range(min(2, n_chunks)):
        _stage_copy(x_hbm, stage_ref, sems, row0, c).start()
    for c in range(n_chunks):
        _stage_copy(x_hbm, stage_ref, sems, row0, c).wait()
        for r0 in range(0, FETCH_ROWS, NORM_ROWS):
            xn_ref[c * FETCH_ROWS + r0:c * FETCH_ROWS + r0 + NORM_ROWS, :] = _rms_rows(
                stage_ref[c % 2, r0:r0 + NORM_ROWS, :], gain_ref[...]).astype(BF16)
        if c + 2 < n_chunks:
            _stage_copy(x_hbm, stage_ref, sems, row0, c + 2).start()


def _params():
    return pltpu.CompilerParams(
        dimension_semantics=("arbitrary", "arbitrary"), vmem_limit_bytes=VMEM_LIMIT)


def _ffn_kernel(x_hbm, gain_ref, wg_ref, wu_ref, wo_ref, fgain_ref, o_ref, xn_ref, sems, *, final_norm):
    j = pl.program_id(1)
    row0 = pl.program_id(0) * o_ref.shape[0]

    @pl.when(j == 0)
    def _():
        _fetch_and_norm(x_hbm, o_ref, sems, row0, gain_ref, xn_ref)

    xn = xn_ref[...]
    g = jnp.dot(xn, wg_ref[...], preferred_element_type=F32)
    u = jnp.dot(xn, wu_ref[...], preferred_element_type=F32)
    a = (0.5 * _silu(g) * u).astype(BF16)
    o_ref[...] += jnp.dot(a, wo_ref[...], preferred_element_type=F32)

    if final_norm:
        @pl.when(j == pl.num_programs(1) - 1)
        def _():
            def body(i, carry):
                r = pl.ds(pl.multiple_of(i * NORM_ROWS, NORM_ROWS), NORM_ROWS)
                o_ref[r, :] = _rms_rows(o_ref[r, :], fgain_ref[...])
                return carry
            lax.fori_loop(0, o_ref.shape[0] // NORM_ROWS, body, 0)


def _ffn(x, gain, w_in, w_out, layer, final_gain, *, final_norm, bm, bf):
    m, d = x.shape
    f = w_out.shape[1]
    nf = f // bf
    kern = functools.partial(_ffn_kernel, final_norm=final_norm)
    return pl.pallas_call(
        kern,
        grid=(m // bm, nf),
        in_specs=[
            pl.BlockSpec(memory_space=pl.ANY),
            pl.BlockSpec((1, d), lambda i, j: (0, 0)),
            pl.BlockSpec((None, d, bf), lambda i, j: (layer, 0, j)),
            pl.BlockSpec((None, d, bf), lambda i, j: (layer, 0, j + nf)),
            pl.BlockSpec((None, bf, d), lambda i, j: (layer, j, 0)),
            pl.BlockSpec((1, d), lambda i, j: (0, 0)),
        ],
        out_specs=pl.BlockSpec((bm, d), lambda i, j: (i, 0)),
        out_shape=jax.ShapeDtypeStruct((m, d), F32),
        scratch_shapes=[pltpu.VMEM((bm, d), BF16), pltpu.SemaphoreType.DMA((bm // FETCH_ROWS,))],
        compiler_params=_params(),
        name="ffn_final" if final_norm else "ffn",
    )(x, gain.reshape(1, d), w_in, w_in, w_out, final_gain.reshape(1, d))


def _cast_job_specs(jobs, grid):
    steps = grid[0] * grid[1]
    in_specs, out_specs, out_shapes = [], [], []
    for w, layer in jobs:
        _, r, c = w.shape
        rb = _cdiv(_cdiv(r, steps), BF16_ROWS) * BF16_ROWS
        last = _cdiv(r, rb) - 1

        def block(i, j, last=last):
            return jnp.minimum(i * grid[1] + j, last)

        in_specs.append(pl.BlockSpec((None, rb, c), lambda i, j, layer=layer, block=block: (layer, block(i, j), 0)))
        out_specs.append(pl.BlockSpec((None, rb, c), lambda i, j, block=block: (0, block(i, j), 0)))
        out_shapes.append(jax.ShapeDtypeStruct((1, r, c), BF16))
    return in_specs, out_specs, out_shapes


def _run_cast_jobs(src_refs, dst_refs):
    for src, dst in zip(src_refs, dst_refs):
        dst[...] = src[...].astype(BF16)


def _inproj_kernel(*refs, n_jobs):
    x_hbm, gain_ref, w_ref = refs[:3]
    job_src = refs[3:3 + n_jobs]
    o_ref = refs[3 + n_jobs]
    job_dst = refs[4 + n_jobs:4 + 2 * n_jobs]
    stage_ref, xn_ref, sems = refs[4 + 2 * n_jobs:]
    row0 = pl.program_id(0) * xn_ref.shape[0]

    @pl.when(pl.program_id(1) == 0)
    def _():
        _stream_and_norm(x_hbm, stage_ref, sems, row0, gain_ref, xn_ref)

    o_ref[...] = jnp.dot(xn_ref[...], w_ref[...], preferred_element_type=F32).astype(o_ref.dtype)
    _run_cast_jobs(job_src, job_dst)


def _inproj(x, gain, w, layer, cast_jobs, *, bm, bn):
    m, d = x.shape
    n = w.shape[2]
    grid = (m // bm, n // bn)
    job_in, job_out, job_shapes = _cast_job_specs(cast_jobs, grid)
    outs = pl.pallas_call(
        functools.partial(_inproj_kernel, n_jobs=len(cast_jobs)),
        grid=grid,
        in_specs=[
            pl.BlockSpec(memory_space=pl.ANY),
            pl.BlockSpec((1, d), lambda i, j: (0, 0)),
            pl.BlockSpec((None, d, bn), lambda i, j: (layer, 0, j)),
        ] + job_in,
        out_specs=[pl.BlockSpec((bm, bn), lambda i, j: (i, j))] + job_out,
        out_shape=[jax.ShapeDtypeStruct((m, n), BF16)] + job_shapes,
        scratch_shapes=[pltpu.VMEM((2, FETCH_ROWS, d), F32), pltpu.VMEM((bm, d), BF16),
                        pltpu.SemaphoreType.DMA((2,))],
        compiler_params=_params(),
        name="inproj",
    )(x, gain.reshape(1, d), w, *[wj for wj, _ in cast_jobs])
    return outs[0], list(outs[1:])


def _spread_taps(w_ref, wb_ref):
    for j in range(w_ref.shape[0]):
        wb_ref[j * SUBLANES:(j + 1) * SUBLANES, :] = jnp.broadcast_to(
            w_ref[j:j + 1, :], (SUBLANES, w_ref.shape[1]))


def _causal_conv_block(buf, wb_ref, r0, cs, halo, ktaps):
    span = -(-(ktaps - 1) // SUBLANES) * SUBLANES
    base = halo + r0 - span
    win = buf[base:base + span + CONV_ROWS, cs]
    lead = span - (ktaps - 1)
    groups = CONV_ROWS // SUBLANES
    acc = None
    for phase in range(SUBLANES):
        taps = [j for j in range(ktaps) if (lead + j) % SUBLANES == phase]
        if not taps:
            continue
        shifted = win[phase:phase + span + CONV_ROWS - SUBLANES] if phase else win
        for j in taps:
            a0 = lead + j - phase
            w = wb_ref[j * SUBLANES:(j + 1) * SUBLANES, cs]
            term = shifted[a0:a0 + CONV_ROWS].reshape(groups, SUBLANES, LANES) * w[None]
            acc = term if acc is None else acc + term
    return acc.reshape(CONV_ROWS, LANES)


def _mixer_kernel(*refs, ts, cw, hd, nq, n_jobs):
    (scb_ref, scc_ref, sch_ref, cfv_ref, cfg_ref) = refs[:5]
    q_refs = refs[5:5 + nq]
    k_refs = refs[5 + nq:5 + 2 * nq]
    v_refs = refs[5 + 2 * nq:5 + 3 * nq]
    g_refs = refs[5 + 3 * nq:5 + 4 * nq]
    rest = refs[5 + 4 * nq:]
    (pos_ref, invf_ref, scw_ref, cfw_ref, cfb_ref, lnw_ref, lnb_ref, gnw_ref) = rest[:8]
    job_src = rest[8:8 + n_jobs]
    o_ref = rest[8 + n_jobs]
    job_dst = rest[9 + n_jobs:9 + 2 * n_jobs]
    sbuf, cbuf, state, scwb, cfwb = rest[9 + 2 * n_jobs:]
    half = hd // 2
    heads_per_ref = cw // hd
    _run_cast_jobs(job_src, job_dst)

    @pl.when(pl.program_id(1) == 0)
    def _():
        sbuf[0:SC_HALO, :] = jnp.zeros((SC_HALO, cw), F32)
        cbuf[0:CF_HALO, :] = jnp.zeros((CF_HALO, cw), F32)
        state[...] = jnp.zeros(state.shape, F32)
        _spread_taps(scw_ref, scwb)
        _spread_taps(cfw_ref, cfwb)

    sbuf[SC_HALO:SC_HALO + ts, :] = scc_ref[...].astype(F32) * sch_ref[...].astype(F32)
    for r0 in range(0, ts, CONV_ROWS):
        rows = slice(r0, r0 + CONV_ROWS)
        for c0 in range(0, cw, LANES):
            cs = slice(c0, c0 + LANES)
            y = _causal_conv_block(sbuf, scwb, r0, cs, SC_HALO, SC_KERNEL)
            o_ref[rows, cs] = (scb_ref[rows, cs].astype(F32) * y).astype(o_ref.dtype)
    sbuf[0:SC_HALO, :] = sbuf[ts:ts + SC_HALO, :]

    cbuf[CF_HALO:CF_HALO + ts, :] = cfv_ref[...].astype(F32) * _sigmoid(cfg_ref[...].astype(F32))
    for r0 in range(0, ts, CONV_ROWS):
        zs = []
        for c0 in range(0, cw, LANES):
            cs = slice(c0, c0 + LANES)
            zs.append(_causal_conv_block(cbuf, cfwb, r0, cs, CF_HALO, CF_KERNEL) + cfb_ref[:, cs])
        z = jnp.concatenate(zs, axis=-1)
        mu = jnp.mean(z, axis=-1, keepdims=True)
        zc = z - mu
        var = jnp.mean(zc * zc, axis=-1, keepdims=True)
        y = zc * lax.rsqrt(var + EPS) * lnw_ref[...] + lnb_ref[...]
        o_ref[r0:r0 + CONV_ROWS, cw:2 * cw] = _silu(y).astype(o_ref.dtype)
    cbuf[0:CF_HALO, :] = cbuf[ts:ts + CF_HALO, :]

    ang = pos_ref[...] * invf_ref[...]
    cos = jnp.cos(ang)
    sin = jnp.sin(ang)
    c = RET_CHUNK
    row = lax.broadcasted_iota(jnp.int32, (c, c), 0).astype(F32)
    col = lax.broadcasted_iota(jnp.int32, (c, c), 1).astype(F32)
    rel = row - col
    rowd = lax.broadcasted_iota(jnp.int32, (c, hd), 0).astype(F32)
    k_scale = float(hd) ** -0.5

    def rot(t, cs_, sn_):
        t1, t2 = t[:, :half], t[:, half:]
        return jnp.concatenate([t1 * cs_ - t2 * sn_, t1 * sn_ + t2 * cs_], axis=-1)

    for h in range(RET_HEADS):
        lg = math.log(1.0 - 2.0 ** (-5.0 - h))
        mask = jnp.exp(jnp.where(rel >= 0, rel * lg, -jnp.inf))
        qd = jnp.exp((rowd + 1.0) * lg)
        kd = jnp.exp((c - 1.0 - rowd) * lg)
        cd = math.exp(c * lg)
        ri, hl = divmod(h, heads_per_ref)
        hs = slice(hl * hd, (hl + 1) * hd)
        for ci in range(ts // c):
            rs = slice(ci * c, (ci + 1) * c)
            cs_, sn_ = cos[rs], sin[rs]
            q = rot(q_refs[ri][rs, hs].astype(F32), cs_, sn_)
            k = rot(k_refs[ri][rs, hs].astype(F32), cs_, sn_) * k_scale
            v = v_refs[ri][rs, hs]
            qb = q.astype(BF16)
            st = state[h]
            scores = lax.dot_general(qb, k.astype(BF16), (((1,), (1,)), ((), ())),
                                     preferred_element_type=F32) * mask
            intra = jnp.dot(scores.astype(BF16), v, preferred_element_type=F32)
            inter = jnp.dot(qb, st.astype(BF16), preferred_element_type=F32) * qd
            state[h] = st * cd + lax.dot_general((k * kd).astype(BF16), v, (((0,), (0,)), ((), ())),
                                                 preferred_element_type=F32)
            o = intra + inter
            mu = jnp.mean(o, axis=-1, keepdims=True)
            oc = o - mu
            var = jnp.mean(oc * oc, axis=-1, keepdims=True)
            on = oc * lax.rsqrt(var + EPS) * gnw_ref[:, h * hd:(h + 1) * hd]
            gate = g_refs[ri][rs, hs].astype(F32)
            o_ref[rs, 2 * cw + h * hd:2 * cw + (h + 1) * hd] = (_silu(gate) * on).astype(o_ref.dtype)


def _mixer(h3, posb, invf, sc_w, cf_w, cf_b, ln_w, ln_b, gn_w, cast_jobs, *, col0, ts, cw, rw):
    b, s, n = h3.shape
    hd = rw // RET_HEADS
    nq = rw // cw
    ncol = 5 + 4 * nq
    cb0 = col0 // cw

    def hspec(col):
        return pl.BlockSpec((None, ts, cw), lambda bi, si: (bi, si, cb0 + col))

    def full(a):
        return pl.BlockSpec(a.shape, lambda bi, si: (0,) * a.ndim)

    small = [invf, sc_w, cf_w, cf_b, ln_w, ln_b, gn_w]
    grid = (b, s // ts)
    job_in, job_out, job_shapes = _cast_job_specs(cast_jobs, grid)
    kern = functools.partial(_mixer_kernel, ts=ts, cw=cw, hd=hd, nq=nq, n_jobs=len(cast_jobs))
    outs = pl.pallas_call(
        kern,
        grid=grid,
        in_specs=[hspec(cidx) for cidx in range(ncol)]
        + [pl.BlockSpec((None, ts, LANES), lambda bi, si: (bi, si, 0))]
        + [full(a) for a in small] + job_in,
        out_specs=[pl.BlockSpec((None, ts, 2 * cw + rw), lambda bi, si: (bi, si, 0))] + job_out,
        out_shape=[jax.ShapeDtypeStruct((b, s, 2 * cw + rw), BF16)] + job_shapes,
        scratch_shapes=[
            pltpu.VMEM((SC_HALO + ts, cw), F32),
            pltpu.VMEM((CF_HALO + ts, cw), F32),
            pltpu.VMEM((RET_HEADS, hd, hd), F32),
            pltpu.VMEM((SC_KERNEL * SUBLANES, cw), F32),
            pltpu.VMEM((CF_KERNEL * SUBLANES, cw), F32),
        ],
        compiler_params=_params(),
        name="mixer",
    )(*([h3] * ncol), posb, *small, *[wj for wj, _ in cast_jobs])
    return outs[0], list(outs[1:])


def _merge_kernel(a_ref, gsc_ref, gcf_ref, gret_ref, wsc_ref, wcf_ref, wret_ref, wmix_ref, x_ref,
                  o_ref, mg_ref, *, cw):
    j = pl.program_id(1)
    nt, _, bn = mg_ref.shape

    @pl.when(j < nt)
    def _():
        y_sc = jnp.dot(a_ref[:, 0:cw], wsc_ref[...], preferred_element_type=F32)
        y_cf = jnp.dot(a_ref[:, cw:2 * cw], wcf_ref[...], preferred_element_type=F32)
        y_ret = jnp.dot(a_ref[:, 2 * cw:], wret_ref[...], preferred_element_type=F32)
        merged = (_sigmoid(gsc_ref[...].astype(F32)) * y_sc + _sigmoid(gcf_ref[...].astype(F32)) * y_cf
                  + _sigmoid(gret_ref[...].astype(F32)) * y_ret)
        mg_ref[j] = merged.astype(BF16)

    @pl.when(j >= nt)
    def _():
        acc = x_ref[...]
        for c in range(nt):
            acc = acc + jnp.dot(mg_ref[c], wmix_ref[c * bn:(c + 1) * bn, :], preferred_element_type=F32)
        o_ref[...] = acc


def _merge(x, a, h, w_sc, w_cf, w_ret, w_mix, layer, *, gate_col0, bm, bn):
    m, d = x.shape
    cw = w_sc.shape[1]
    rw = w_ret.shape[1]
    nt = d // bn
    g0 = gate_col0 // bn

    def build(j):
        return jnp.minimum(j, nt - 1)

    def emit(j):
        return jnp.maximum(j - nt, 0)

    def gspec(kidx):
        return pl.BlockSpec((bm, bn), lambda i, j: (i, g0 + kidx * nt + build(j)))

    def wspec(k):
        return pl.BlockSpec((None, k, bn), lambda i, j: (layer, 0, build(j)))

    kern = functools.partial(_merge_kernel, cw=cw)
    return pl.pallas_call(
        kern,
        grid=(m // bm, 2 * nt),
        in_specs=[
            pl.BlockSpec((bm, 2 * cw + rw), lambda i, j: (i, 0)),
            gspec(0), gspec(1), gspec(2),
            wspec(cw), wspec(cw), wspec(rw),
            pl.BlockSpec((None, d, bn), lambda i, j: (layer, 0, emit(j))),
            pl.BlockSpec((bm, bn), lambda i, j: (i, emit(j))),
        ],
        out_specs=pl.BlockSpec((bm, bn), lambda i, j: (i, emit(j))),
        out_shape=jax.ShapeDtypeStruct((m, d), F32),
        scratch_shapes=[pltpu.VMEM((nt, bm, bn), BF16)],
        compiler_params=_params(),
        name="merge",
    )(a, h, h, h, w_sc, w_cf, w_ret, w_mix, x)


def _tile(n, pref):
    return pref if n % pref == 0 else n


def kernel(x, positions, norm_ffn1, ffn1_in, ffn1_out, norm_mix, w_in, sc_conv_w, cf_dw_w, cf_dw_b, cf_ln_w, cf_ln_b, ret_gn_w, w_sc_out, w_cf_out, w_ret_out, w_mix_out, norm_ffn2, ffn2_in, ffn2_out, norm_final):
    b, s, d = x.shape
    m = b * s
    depth = w_in.shape[0]
    cw = w_sc_out.shape[1]
    rw = w_ret_out.shape[1]
    hd = rw // RET_HEADS
    half = hd // 2
    assert w_cf_out.shape[1] == cw and rw % cw == 0 and half == LANES
    gate_col0 = 5 * cw + 4 * rw
    assert w_in.shape[2] == gate_col0 + 3 * d

    bm_ffn = _tile(m, 1024)
    bf = _tile(ffn1_out.shape[1], 256)
    bm_in = _tile(m, 1024)
    bn_in = _tile(w_in.shape[2], 1024)
    bm_mg = _tile(m, 1024)
    bn_mg = _tile(d, 512)
    ts = _tile(s, 256)

    inv_freq = (ROPE_BASE ** (-jnp.arange(half, dtype=F32) / half)).reshape(1, half)
    posb = jnp.broadcast_to(positions.astype(F32)[..., None], (b, s, LANES))

    w_in, w_sc_out, w_cf_out, w_ret_out, w_mix_out = (
        w.astype(BF16) for w in (w_in, w_sc_out, w_cf_out, w_ret_out, w_mix_out))
    ffn1 = (ffn1_in[:1].astype(BF16), ffn1_out[:1].astype(BF16))

    xf = x.reshape(m, d)
    for l in range(depth):
        xf = _ffn(xf, norm_ffn1[l], ffn1[0], ffn1[1], 0, norm_final, final_norm=False, bm=bm_ffn, bf=bf)
        nxt = l + 1 < depth
        h, cast_in = _inproj(xf, norm_mix[l], w_in, l,
                             [(ffn2_in, l)] + ([(ffn1_in, l + 1), (ffn1_out, l + 1)] if nxt else []),
                             bm=bm_in, bn=bn_in)
        a, cast_out = _mixer(h.reshape(b, s, -1), posb, inv_freq, sc_conv_w[l], cf_dw_w[l],
                             cf_dw_b[l].reshape(1, cw), cf_ln_w[l].reshape(1, cw),
                             cf_ln_b[l].reshape(1, cw), ret_gn_w[l].reshape(1, rw),
                             [(ffn2_out, l)], col0=0, ts=ts, cw=cw, rw=rw)
        xf = _merge(xf, a.reshape(m, -1), h, w_sc_out, w_cf_out, w_ret_out, w_mix_out, l,
                    gate_col0=gate_col0, bm=bm_mg, bn=bn_mg)
        xf = _ffn(xf, norm_ffn2[l], cast_in[0], cast_out[0], 0, norm_final,
                  final_norm=not nxt, bm=bm_ffn, bf=bf)
        if nxt:
            ffn1 = (cast_in[1], cast_in[2])
    return xf.reshape(b, s, d)
```

```python
import functools
import math

import jax
import jax.numpy as jnp
from jax import lax
from jax.experimental import pallas as pl
from jax.experimental.pallas import tpu as pltpu

EPS = 1e-6
ROPE_BASE = 10000.0
RET_HEADS = 8
RET_CHUNK = 128
SC_KERNEL = 3
CF_KERNEL = 31
SUBLANES = 8
LANES = 128
BF16_ROWS = 16
SC_HALO = 8
CF_HALO = 32
CONV_ROWS = 64
NORM_ROWS = 64
FETCH_ROWS = 128
VMEM_LIMIT = 58 * 1024 * 1024

F32 = jnp.float32
BF16 = jnp.bfloat16


def _cdiv(a, b):
    return -(-a // b)


def _sigmoid(x):
    return 1.0 / (1.0 + jnp.exp(-x))


def _silu(x):
    return x * _sigmoid(x)


def _rms_rows(x, gain):
    return (x * lax.rsqrt(jnp.mean(x * x, axis=-1, keepdims=True) + EPS)) * gain


def _row_copy(x_hbm, dst_ref, sems, row0, c):
    return pltpu.make_async_copy(
        x_hbm.at[pl.ds(row0 + c * FETCH_ROWS, FETCH_ROWS), :],
        dst_ref.at[pl.ds(c * FETCH_ROWS, FETCH_ROWS), :],
        sems.at[c])


def _fetch_rows_start(x_hbm, dst_ref, sems, row0):
    for c in range(dst_ref.shape[0] // FETCH_ROWS):
        _row_copy(x_hbm, dst_ref, sems, row0, c).start()


def _fetch_and_norm(x_hbm, dst_ref, sems, row0, gain_ref, xn_ref):
    _fetch_rows_start(x_hbm, dst_ref, sems, row0)
    for c in range(dst_ref.shape[0] // FETCH_ROWS):
        _row_copy(x_hbm, dst_ref, sems, row0, c).wait()
        for r0 in range(c * FETCH_ROWS, (c + 1) * FETCH_ROWS, NORM_ROWS):
            rows = slice(r0, r0 + NORM_ROWS)
            xn_ref[rows, :] = _rms_rows(dst_ref[rows, :], gain_ref[...]).astype(BF16)


def _stage_copy(x_hbm, stage_ref, sems, row0, c):
    slot = c % 2
    return pltpu.make_async_copy(
        x_hbm.at[pl.ds(row0 + c * FETCH_ROWS, FETCH_ROWS), :], stage_ref.at[slot], sems.at[slot])


def _stream_and_norm(x_hbm, stage_ref, sems, row0, gain_ref, xn_ref):
    n_chunks = xn_ref.shape[0] // FETCH_ROWS
    for c in range(min(2, n_chunks)):
        _stage_copy(x_hbm, stage_ref, sems, row0, c).start()
    for c in range(n_chunks):
        _stage_copy(x_hbm, stage_ref, sems, row0, c).wait()
        for r0 in range(0, FETCH_ROWS, NORM_ROWS):
            xn_ref[c * FETCH_ROWS + r0:c * FETCH_ROWS + r0 + NORM_ROWS, :] = _rms_rows(
                stage_ref[c % 2, r0:r0 + NORM_ROWS, :], gain_ref[...]).astype(BF16)
        if c + 2 < n_chunks:
            _stage_copy(x_hbm, stage_ref, sems, row0, c + 2).start()


def _params():
    return pltpu.CompilerParams(
        dimension_semantics=("arbitrary", "arbitrary"), vmem_limit_bytes=VMEM_LIMIT)


def _ffn_kernel(x_hbm, gain_ref, wg_ref, wu_ref, wo_ref, fgain_ref, o_ref, xn_ref, sems, *, final_norm):
    j = pl.program_id(1)
    row0 = pl.program_id(0) * o_ref.shape[0]

    @pl.when(j == 0)
    def _():
        _fetch_and_norm(x_hbm, o_ref, sems, row0, gain_ref, xn_ref)

    xn = xn_ref[...]
    g = jnp.dot(xn, wg_ref[...], preferred_element_type=F32)
    u = jnp.dot(xn, wu_ref[...], preferred_element_type=F32)
    a = (0.5 * _silu(g) * u).astype(BF16)
    o_ref[...] += jnp.dot(a, wo_ref[...], preferred_element_type=F32)

    if final_norm:
        @pl.when(j == pl.num_programs(1) - 1)
        def _():
            def body(i, carry):
                r = pl.ds(pl.multiple_of(i * NORM_ROWS, NORM_ROWS), NORM_ROWS)
                o_ref[r, :] = _rms_rows(o_ref[r, :], fgain_ref[...])
                return carry
            lax.fori_loop(0, o_ref.shape[0] // NORM_ROWS, body, 0)


def _ffn(x, gain, w_in, w_out, layer, final_gain, *, final_norm, bm, bf):
    m, d = x.shape
    f = w_out.shape[1]
    nf = f // bf
    kern = functools.partial(_ffn_kernel, final_norm=final_norm)
    return pl.pallas_call(
        kern,
        grid=(m // bm, nf),
        in_specs=[
            pl.BlockSpec(memory_space=pl.ANY),
            pl.BlockSpec((1, d), lambda i, j: (0, 0)),
            pl.BlockSpec((None, d, bf), lambda i, j: (layer, 0, j)),
            pl.BlockSpec((None, d, bf), lambda i, j: (layer, 0, j + nf)),
            pl.BlockSpec((None, bf, d), lambda i, j: (layer, j, 0)),
            pl.BlockSpec((1, d), lambda i, j: (0, 0)),
        ],
        out_specs=pl.BlockSpec((bm, d), lambda i, j: (i, 0)),
        out_shape=jax.ShapeDtypeStruct((m, d), F32),
        scratch_shapes=[pltpu.VMEM((bm, d), BF16), pltpu.SemaphoreType.DMA((bm // FETCH_ROWS,))],
        compiler_params=_params(),
        name="ffn_final" if final_norm else "ffn",
    )(x, gain.reshape(1, d), w_in, w_in, w_out, final_gain.reshape(1, d))


def _cast_job_specs(jobs, grid):
    steps = grid[0] * grid[1]
    in_specs, out_specs, out_shapes = [], [], []
    for w, layer in jobs:
        _, r, c = w.shape
        rb = _cdiv(_cdiv(r, steps), BF16_ROWS) * BF16_ROWS
        last = _cdiv(r, rb) - 1

        def block(i, j, last=last):
            return jnp.minimum(i * grid[1] + j, last)

        in_specs.append(pl.BlockSpec((None, rb, c), lambda i, j, layer=layer, block=block: (layer, block(i, j), 0)))
        out_specs.append(pl.BlockSpec((None, rb, c), lambda i, j, block=block: (0, block(i, j), 0)))
        out_shapes.append(jax.ShapeDtypeStruct((1, r, c), BF16))
    return in_specs, out_specs, out_shapes


def _run_cast_jobs(src_refs, dst_refs):
    for src, dst in zip(src_refs, dst_refs):
        dst[...] = src[...].astype(BF16)


def _inproj_kernel(*refs, n_jobs):
    x_hbm, gain_ref, w_ref = refs[:3]
    job_src = refs[3:3 + n_jobs]
    o_ref = refs[3 + n_jobs]
    job_dst = refs[4 + n_jobs:4 + 2 * n_jobs]
    stage_ref, xn_ref, sems = refs[4 + 2 * n_jobs:]
    row0 = pl.program_id(0) * xn_ref.shape[0]

    @pl.when(pl.program_id(1) == 0)
    def _():
        _stream_and_norm(x_hbm, stage_ref, sems, row0, gain_ref, xn_ref)

    o_ref[...] = jnp.dot(xn_ref[...], w_ref[...], preferred_element_type=F32).astype(o_ref.dtype)
    _run_cast_jobs(job_src, job_dst)


def _inproj(x, gain, w, layer, cast_jobs, *, bm, bn):
    m, d = x.shape
    n = w.shape[2]
    grid = (m // bm, n // bn)
    job_in, job_out, job_shapes = _cast_job_specs(cast_jobs, grid)
    outs = pl.pallas_call(
        functools.partial(_inproj_kernel, n_jobs=len(cast_jobs)),
        grid=grid,
        in_specs=[
            pl.BlockSpec(memory_space=pl.ANY),
            pl.BlockSpec((1, d), lambda i, j: (0, 0)),
            pl.BlockSpec((None, d, bn), lambda i, j: (layer, 0, j)),
        ] + job_in,
        out_specs=[pl.BlockSpec((bm, bn), lambda i, j: (i, j))] + job_out,
        out_shape=[jax.ShapeDtypeStruct((m, n), BF16)] + job_shapes,
        scratch_shapes=[pltpu.VMEM((2, FETCH_ROWS, d), F32), pltpu.VMEM((bm, d), BF16),
                        pltpu.SemaphoreType.DMA((2,))],
        compiler_params=_params(),
        name="inproj",
    )(x, gain.reshape(1, d), w, *[wj for wj, _ in cast_jobs])
    return outs[0], list(outs[1:])


def _spread_taps(w_ref, wb_ref):
    for j in range(w_ref.shape[0]):
        wb_ref[j * SUBLANES:(j + 1) * SUBLANES, :] = jnp.broadcast_to(
            w_ref[j:j + 1, :], (SUBLANES, w_ref.shape[1]))


def _causal_conv_block(buf, wb_ref, r0, cs, halo, ktaps):
    span = -(-(ktaps - 1) // SUBLANES) * SUBLANES
    base = halo + r0 - span
    win = buf[base:base + span + CONV_ROWS, cs]
    lead = span - (ktaps - 1)
    groups = CONV_ROWS // SUBLANES
    acc = None
    for phase in range(SUBLANES):
        taps = [j for j in range(ktaps) if (lead + j) % SUBLANES == phase]
        if not taps:
            continue
        shifted = win[phase:phase + span + CONV_ROWS - SUBLANES] if phase else win
        for j in taps:
            a0 = lead + j - phase
            w = wb_ref[j * SUBLANES:(j + 1) * SUBLANES, cs]
            term = shifted[a0:a0 + CONV_ROWS].reshape(groups, SUBLANES, LANES) * w[None]
            acc = term if acc is None else acc + term
    return acc.reshape(CONV_ROWS, LANES)


def _mixer_kernel(*refs, ts, cw, hd, nq, n_jobs):
    (scb_ref, scc_ref, sch_ref, cfv_ref, cfg_ref) = refs[:5]
    q_refs = refs[5:5 + nq]
    k_refs = refs[5 + nq:5 + 2 * nq]
    v_refs = refs[5 + 2 * nq:5 + 3 * nq]
    g_refs = refs[5 + 3 * nq:5 + 4 * nq]
    rest = refs[5 + 4 * nq:]
    (pos_ref, invf_ref, scw_ref, cfw_ref, cfb_ref, lnw_ref, lnb_ref, gnw_ref) = rest[:8]
    job_src = rest[8:8 + n_jobs]
    o_ref = rest[8 + n_jobs]
    job_dst = rest[9 + n_jobs:9 + 2 * n_jobs]
    sbuf, cbuf, state, scwb, cfwb = rest[9 + 2 * n_jobs:]
    half = hd // 2
    heads_per_ref = cw // hd
    _run_cast_jobs(job_src, job_dst)

    @pl.when(pl.program_id(1) == 0)
    def _():
        sbuf[0:SC_HALO, :] = jnp.zeros((SC_HALO, cw), F32)
        cbuf[0:CF_HALO, :] = jnp.zeros((CF_HALO, cw), F32)
        state[...] = jnp.zeros(state.shape, F32)
        _spread_taps(scw_ref, scwb)
        _spread_taps(cfw_ref, cfwb)

    sbuf[SC_HALO:SC_HALO + ts, :] = scc_ref[...].astype(F32) * sch_ref[...].astype(F32)
    for r0 in range(0, ts, CONV_ROWS):
        rows = slice(r0, r0 + CONV_ROWS)
        for c0 in range(0, cw, LANES):
            cs = slice(c0, c0 + LANES)
            y = _causal_conv_block(sbuf, scwb, r0, cs, SC_HALO, SC_KERNEL)
            o_ref[rows, cs] = (scb_ref[rows, cs].astype(F32) * y).astype(o_ref.dtype)
    sbuf[0:SC_HALO, :] = sbuf[ts:ts + SC_HALO, :]

    cbuf[CF_HALO:CF_HALO + ts, :] = cfv_ref[...].astype(F32) * _sigmoid(cfg_ref[...].astype(F32))
    for r0 in range(0, ts, CONV_ROWS):
        zs = []
        for c0 in range(0, cw, LANES):
            cs = slice(c0, c0 + LANES)
            zs.append(_causal_conv_block(cbuf, cfwb, r0, cs, CF_HALO, CF_KERNEL) + cfb_ref[:, cs])
        z = jnp.concatenate(zs, axis=-1)
        mu = jnp.mean(z, axis=-1, keepdims=True)
        zc = z - mu
        var = jnp.mean(zc * zc, axis=-1, keepdims=True)
        y = zc * lax.rsqrt(var + EPS) * lnw_ref[...] + lnb_ref[...]
        o_ref[r0:r0 + CONV_ROWS, cw:2 * cw] = _silu(y).astype(o_ref.dtype)
    cbuf[0:CF_HALO, :] = cbuf[ts:ts + CF_HALO, :]

    ang = pos_ref[...] * invf_ref[...]
    cos = jnp.cos(ang)
    sin = jnp.sin(ang)
    c = RET_CHUNK
    row = lax.broadcasted_iota(jnp.int32, (c, c), 0).astype(F32)
    col = lax.broadcasted_iota(jnp.int32, (c, c), 1).astype(F32)
    rel = row - col
    rowd = lax.broadcasted_iota(jnp.int32, (c, hd), 0).astype(F32)
    k_scale = float(hd) ** -0.5

    def rot(t, cs_, sn_):
        t1, t2 = t[:, :half], t[:, half:]
        return jnp.concatenate([t1 * cs_ - t2 * sn_, t1 * sn_ + t2 * cs_], axis=-1)

    for h in range(RET_HEADS):
        lg = math.log(1.0 - 2.0 ** (-5.0 - h))
        mask = jnp.exp(jnp.where(rel >= 0, rel * lg, -jnp.inf))
        qd = jnp.exp((rowd + 1.0) * lg)
        kd = jnp.exp((c - 1.0 - rowd) * lg)
        cd = math.exp(c * lg)
        ri, hl = divmod(h, heads_per_ref)
        hs = slice(hl * hd, (hl + 1) * hd)
        for ci in range(ts // c):
            rs = slice(ci * c, (ci + 1) * c)
            cs_, sn_ = cos[rs], sin[rs]
            q = rot(q_refs[ri][rs, hs].astype(F32), cs_, sn_)
            k = rot(k_refs[ri][rs, hs].astype(F32), cs_, sn_) * k_scale
            v = v_refs[ri][rs, hs]
            qb = q.astype(BF16)
            st = state[h]
            scores = lax.dot_general(qb, k.astype(BF16), (((1,), (1,)), ((), ())),
                                     preferred_element_type=F32) * mask
            intra = jnp.dot(scores.astype(BF16), v, preferred_element_type=F32)
            inter = jnp.dot(qb, st.astype(BF16), preferred_element_type=F32) * qd
            state[h] = st * cd + lax.dot_general((k * kd).astype(BF16), v, (((0,), (0,)), ((), ())),
                                                 preferred_element_type=F32)
            o = intra + inter
            mu = jnp.mean(o, axis=-1, keepdims=True)
            oc = o - mu
            var = jnp.mean(oc * oc, axis=-1, keepdims=True)
            on = oc * lax.rsqrt(var + EPS) * gnw_ref[:, h * hd:(h + 1) * hd]
            gate = g_refs[ri][rs, hs].astype(F32)
            o_ref[rs, 2 * cw + h * hd:2 * cw + (h + 1) * hd] = (_silu(gate) * on).astype(o_ref.dtype)


def _mixer(h3, posb, invf, sc_w, cf_w, cf_b, ln_w, ln_b, gn_w, cast_jobs, *, col0, ts, cw, rw):
    b, s, n = h3.shape
    hd = rw // RET_HEADS
    nq = rw // cw
    ncol = 5 + 4 * nq
    cb0 = col0 // cw

    def hspec(col):
        return pl.BlockSpec((None, ts, cw), lambda bi, si: (bi, si, cb0 + col))

    def full(a):
        return pl.BlockSpec(a.shape, lambda bi, si: (0,) * a.ndim)

    small = [invf, sc_w, cf_w, cf_b, ln_w, ln_b, gn_w]
    grid = (b, s // ts)
    job_in, job_out, job_shapes = _cast_job_specs(cast_jobs, grid)
    kern = functools.partial(_mixer_kernel, ts=ts, cw=cw, hd=hd, nq=nq, n_jobs=len(cast_jobs))
    outs = pl.pallas_call(
        kern,
        grid=grid,
        in_specs=[hspec(cidx) for cidx in range(ncol)]
        + [pl.BlockSpec((None, ts, LANES), lambda bi, si: (bi, si, 0))]
        + [full(a) for a in small] + job_in,
        out_specs=[pl.BlockSpec((None, ts, 2 * cw + rw), lambda bi, si: (bi, si, 0))] + job_out,
        out_shape=[jax.ShapeDtypeStruct((b, s, 2 * cw + rw), BF16)] + job_shapes,
        scratch_shapes=[
            pltpu.VMEM((SC_HALO + ts, cw), F32),
            pltpu.VMEM((CF_HALO + ts, cw), F32),
            pltpu.VMEM((RET_HEADS, hd, hd), F32),
            pltpu.VMEM((SC_KERNEL * SUBLANES, cw), F32),
            pltpu.VMEM((CF_KERNEL * SUBLANES, cw), F32),
        ],
        compiler_params=_params(),
        name="mixer",
    )(*([h3] * ncol), posb, *small, *[wj for wj, _ in cast_jobs])
    return outs[0], list(outs[1:])


def _merge_kernel(a_ref, gsc_ref, gcf_ref, gret_ref, wsc_ref, wcf_ref, wret_ref, wmix_ref, x_ref,
                  o_ref, mg_ref, *, cw):
    j = pl.program_id(1)
    nt, _, bn = mg_ref.shape

    @pl.when(j < nt)
    def _():
        y_sc = jnp.dot(a_ref[:, 0:cw], wsc_ref[...], preferred_element_type=F32)
        y_cf = jnp.dot(a_ref[:, cw:2 * cw], wcf_ref[...], preferred_element_type=F32)
        y_ret = jnp.dot(a_ref[:, 2 * cw:], wret_ref[...], preferred_element_type=F32)
        merged = (_sigmoid(gsc_ref[...].astype(F32)) * y_sc + _sigmoid(gcf_ref[...].astype(F32)) * y_cf
                  + _sigmoid(gret_ref[...].astype(F32)) * y_ret)
        mg_ref[j] = merged.astype(BF16)

    @pl.when(j >= nt)
    def _():
        acc = x_ref[...]
        for c in range(nt):
            acc = acc + jnp.dot(mg_ref[c], wmix_ref[c * bn:(c + 1) * bn, :], preferred_element_type=F32)
        o_ref[...] = acc


def _merge(x, a, h, w_sc, w_cf, w_ret, w_mix, layer, *, gate_col0, bm, bn):
    m, d = x.shape
    cw = w_sc.shape[1]
    rw = w_ret.shape[1]
    nt = d // bn
    g0 = gate_col0 // bn

    def build(j):
        return jnp.minimum(j, nt - 1)

    def emit(j):
        return jnp.maximum(j - nt, 0)

    def gspec(kidx):
        return pl.BlockSpec((bm, bn), lambda i, j: (i, g0 + kidx * nt + build(j)))

    def wspec(k):
        return pl.BlockSpec((None, k, bn), lambda i, j: (layer, 0, build(j)))

    kern = functools.partial(_merge_kernel, cw=cw)
    return pl.pallas_call(
        kern,
        grid=(m // bm, 2 * nt),
        in_specs=[
            pl.BlockSpec((bm, 2 * cw + rw), lambda i, j: (i, 0)),
            gspec(0), gspec(1), gspec(2),
            wspec(cw), wspec(cw), wspec(rw),
            pl.BlockSpec((None, d, bn), lambda i, j: (layer, 0, emit(j))),
            pl.BlockSpec((bm, bn), lambda i, j: (i, emit(j))),
        ],
        out_specs=pl.BlockSpec((bm, bn), lambda i, j: (i, emit(j))),
        out_shape=jax.ShapeDtypeStruct((m, d), F32),
        scratch_shapes=[pltpu.VMEM((nt, bm, bn), BF16)],
        compiler_params=_params(),
        name="merge",
    )(a, h, h, h, w_sc, w_cf, w_ret, w_mix, x)


def _tile(n, pref):
    return pref if n % pref == 0 else n


def kernel(x, positions, norm_ffn1, ffn1_in, ffn1_out, norm_mix, w_in, sc_conv_w, cf_dw_w, cf_dw_b, cf_ln_w, cf_ln_b, ret_gn_w, w_sc_out, w_cf_out, w_ret_out, w_mix_out, norm_ffn2, ffn2_in, ffn2_out, norm_final):
    b, s, d = x.shape
    m = b * s
    depth = w_in.shape[0]
    cw = w_sc_out.shape[1]
    rw = w_ret_out.shape[1]
    hd = rw // RET_HEADS
    half = hd // 2
    assert w_cf_out.shape[1] == cw and rw % cw == 0 and half == LANES
    gate_col0 = 5 * cw + 4 * rw
    assert w_in.shape[2] == gate_col0 + 3 * d

    bm_ffn = _tile(m, 1024)
    bf = _tile(ffn1_out.shape[1], 256)
    bm_in = _tile(m, 1024)
    bn_in = _tile(w_in.shape[2], 1024)
    bm_mg = _tile(m, 1024)
    bn_mg = _tile(d, 512)
    ts = _tile(s, 256)

    inv_freq = (ROPE_BASE ** (-jnp.arange(half, dtype=F32) / half)).reshape(1, half)
    posb = jnp.broadcast_to(positions.astype(F32)[..., None], (b, s, LANES))

    w_sc_out, w_cf_out, w_ret_out, w_mix_out = (
        w.astype(BF16) for w in (w_sc_out, w_cf_out, w_ret_out, w_mix_out))
    ffn1 = (ffn1_in[:1].astype(BF16), ffn1_out[:1].astype(BF16))
    w_in_l = w_in[:1].astype(BF16)

    xf = x.reshape(m, d)
    for l in range(depth):
        xf = _ffn(xf, norm_ffn1[l], ffn1[0], ffn1[1], 0, norm_final, final_norm=False, bm=bm_ffn, bf=bf)
        nxt = l + 1 < depth
        h, cast_in = _inproj(xf, norm_mix[l], w_in_l, 0,
                             [(ffn2_in, l)]
                             + ([(ffn1_in, l + 1), (ffn1_out, l + 1), (w_in, l + 1)] if nxt else []),
                             bm=bm_in, bn=bn_in)
        a, cast_out = _mixer(h.reshape(b, s, -1), posb, inv_freq, sc_conv_w[l], cf_dw_w[l],
                             cf_dw_b[l].reshape(1, cw), cf_ln_w[l].reshape(1, cw),
                             cf_ln_b[l].reshape(1, cw), ret_gn_w[l].reshape(1, rw),
                             [(ffn2_out, l)], col0=0, ts=ts, cw=cw, rw=rw)
        xf = _merge(xf, a.reshape(m, -1), h, w_sc_out, w_cf_out, w_ret_out, w_mix_out, l,
                    gate_col0=gate_col0, bm=bm_mg, bn=bn_mg)
        xf = _ffn(xf, norm_ffn2[l], cast_in[0], cast_out[0], 0, norm_final,
                  final_norm=not nxt, bm=bm_ffn, bf=bf)
        if nxt:
            ffn1 = (cast_in[1], cast_in[2])
            w_in_l = cast_in[3]
    return xf.reshape(b, s, d)
```
